```python
import math
import jax, jax.numpy as jnp
from jax import lax
import numpy as np

D_MODEL = 4096
BATCH = 1
SEQ = 8192
DEPTH = 2

HEAD_DIM_A = 128
N_HEADS_A = (D_MODEL // 2) // HEAD_DIM_A
DILATED_BRANCHES = ((128, 1), (512, 4), (2048, 16))
HEAD_DIM_B = 64
N_Q_HEADS_B = (D_MODEL // 2) // HEAD_DIM_B
N_KV_HEADS_B = N_Q_HEADS_B // 8
GQA_GROUP = N_Q_HEADS_B // N_KV_HEADS_B
WINDOW_B = 128
WIDTH_A = N_HEADS_A * HEAD_DIM_A
WIDTH_B = N_Q_HEADS_B * HEAD_DIM_B
MIX_WIDTH = WIDTH_A + WIDTH_B
KV_WIDTH_B = N_KV_HEADS_B * HEAD_DIM_B
IN_WIDTH = 3 * WIDTH_A + WIDTH_B + 2 * KV_WIDTH_B
N_ALIBI_HEADS = N_HEADS_A + N_Q_HEADS_B
D_FF = 256 * math.ceil(8 * D_MODEL / 3 / 256)
CONV_WIDTH = 3
BLOCK = 128
N_MOD = 6
DEEPNORM_ALPHA = (2 * DEPTH) ** 0.25
DEEPNORM_BETA = (8 * DEPTH) ** -0.25
LN_EPS = 1e-5

kernel_name = "hybrid_dilated_swa_sink_convffn_deepnorm"


def alibi_slopes():
    i = jnp.arange(1, N_ALIBI_HEADS + 1, dtype=jnp.float32)
    return jnp.exp2(-8.0 * i / N_ALIBI_HEADS)


def layer_norm(x, g, b):
    xf = x.astype(jnp.float32)
    mu = xf.mean(-1, keepdims=True)
    var = jnp.square(xf - mu).mean(-1, keepdims=True)
    y = (xf - mu) * lax.rsqrt(var + LN_EPS) * g.astype(jnp.float32) + b.astype(jnp.float32)
    return y.astype(x.dtype)


def banded_attention(q, k, v, slopes, max_dist, dist_scale, sinks=None):
    n, L, hk, g, dh = q.shape
    nb = -(-L // BLOCK)
    pad = nb * BLOCK - L
    q = jnp.pad(q, ((0, 0), (0, pad), (0, 0), (0, 0), (0, 0)))
    k = jnp.pad(k, ((0, 0), (0, pad), (0, 0), (0, 0)))
    v = jnp.pad(v, ((0, 0), (0, pad), (0, 0), (0, 0)))
    qb = q.reshape(n, nb, BLOCK, hk, g, dh)

    def with_prev(t):
        tb = t.reshape(n, nb, BLOCK, hk, dh)
        prev = jnp.pad(tb[:, :-1], ((0, 0), (1, 0), (0, 0), (0, 0), (0, 0)))
        return jnp.concatenate([prev, tb], axis=2)

    kk, vv = with_prev(k), with_prev(v)
    s = jnp.einsum('nbqhgd,nbkhd->nbhgqk', qb, kk,
                   preferred_element_type=jnp.float32) * (dh ** -0.5)
    qpos = jnp.arange(BLOCK)[:, None] + BLOCK
    kpos = jnp.arange(2 * BLOCK)[None, :]
    dist = qpos - kpos
    band = (dist >= 0) & (dist <= max_dist)
    has_prev = (jnp.arange(nb)[:, None, None] > 0) | (kpos >= BLOCK)[None]
    valid = band[None] & has_prev
    bias = -(slopes.astype(jnp.float32) * dist_scale)[:, :, None, None] * dist.astype(jnp.float32)
    s = jnp.where(valid[:, None, None], s + bias, -jnp.inf)
    lse = jax.nn.logsumexp(s, axis=-1)
    if sinks is not None:
        lse = jnp.logaddexp(lse, sinks.astype(jnp.float32)[..., None])
    p = jnp.exp(s - lse[..., None]).astype(v.dtype)
    o = jnp.einsum('nbhgqk,nbkhd->nbqhgd', p, vv).reshape(n, nb * BLOCK, hk, g, dh)[:, :L]
    lse = lse.transpose(0, 1, 4, 2, 3).reshape(n, nb * BLOCK, hk, g)[:, :L]
    return o, lse


def dilated_attention(q, k, v, slopes):
    b, s, h, dh = q.shape
    outs, lses = [], []
    for window, dil in DILATED_BRANCHES:
        L = s // dil

        def to_strided(t):
            return t.reshape(b, L, dil, h, dh).transpose(0, 2, 1, 3, 4).reshape(b * dil, L, h, dh)

        o, lse = banded_attention(to_strided(q)[:, :, :, None], to_strided(k), to_strided(v),
                                  slopes[:, None], window // dil, dil)
        o = o[:, :, :, 0].reshape(b, dil, L, h, dh).transpose(0, 2, 1, 3, 4).reshape(b, s, h, dh)
        lse = lse[..., 0].reshape(b, dil, L, h).transpose(0, 2, 1, 3).reshape(b, s, h)
        outs.append(o)
        lses.append(lse)
    w = jax.nn.softmax(jnp.stack(lses, 0), axis=0)
    return jnp.einsum('rbsh,rbshd->bshd', w.astype(q.dtype), jnp.stack(outs, 0))


def causal_depthwise_conv(h, w, b):
    out = lax.conv_general_dilated(h, w[:, None, :].astype(h.dtype), window_strides=(1,),
                                   padding=[(CONV_WIDTH - 1, 0)],
                                   dimension_numbers=('NWC', 'WIO', 'NWC'),
                                   feature_group_count=h.shape[-1])
    return out + b


def setup_inputs(seed: int = 0) -> dict:
    key = jax.random.key(seed)
    ks = jax.random.split(key, 16)
    f32 = jnp.float32
    nrm = lambda k, shape, sc: jax.random.normal(k, shape, f32) * sc
    beta = DEEPNORM_BETA
    col_scale = np.concatenate([
        np.ones(2 * WIDTH_A), np.full(WIDTH_A, beta),
        np.ones(WIDTH_B + KV_WIDTH_B), np.full(KV_WIDTH_B, beta)
    ]).astype(np.float32)
    return {
        "x": nrm(ks[0], (BATCH, SEQ, D_MODEL), 1.0),
        "c": nrm(ks[1], (BATCH, D_MODEL), 1.0),
        "w_mod": nrm(ks[2], (DEPTH, D_MODEL, N_MOD * D_MODEL), 0.2 * D_MODEL ** -0.5),
        "b_mod": nrm(ks[3], (DEPTH, N_MOD * D_MODEL), 0.02),
        "w_in": nrm(ks[4], (DEPTH, D_MODEL, IN_WIDTH), D_MODEL ** -0.5) * jnp.asarray(col_scale),
        "sinks": nrm(ks[5], (DEPTH, N_Q_HEADS_B), 1.0),
        "w_out": nrm(ks[6], (DEPTH, MIX_WIDTH, D_MODEL), beta * MIX_WIDTH ** -0.5),
        "ln1_g": 1.0 + nrm(ks[7], (DEPTH, D_MODEL), 0.02),
        "ln1_b": nrm(ks[8], (DEPTH, D_MODEL), 0.02),
        "w_up": nrm(ks[9], (DEPTH, D_MODEL, 2 * D_FF), D_MODEL ** -0.5),
        "conv_w": nrm(ks[10], (DEPTH, CONV_WIDTH, 2 * D_FF), CONV_WIDTH ** -0.5),
        "conv_b": nrm(ks[11], (DEPTH, 2 * D_FF), 0.02),
        "w_down": nrm(ks[12], (DEPTH, D_FF, D_MODEL), beta * D_FF ** -0.5),
        "ln2_g": 1.0 + nrm(ks[13], (DEPTH, D_MODEL), 0.02),
        "ln2_b": nrm(ks[14], (DEPTH, D_MODEL), 0.02),
    }


def reference(x, c, w_mod, b_mod, w_in, sinks, w_out, ln1_g, ln1_b,
              w_up, conv_w, conv_b, w_down, ln2_g, ln2_b):
    b, s, _ = x.shape
    slopes = alibi_slopes()
    slopes_b = slopes[:N_Q_HEADS_B].reshape(N_KV_HEADS_B, GQA_GROUP)
    slopes_a = slopes[N_Q_HEADS_B:]
    split_at = [WIDTH_A, 2 * WIDTH_A, 3 * WIDTH_A, 3 * WIDTH_A + WIDTH_B,
                3 * WIDTH_A + WIDTH_B + KV_WIDTH_B]
    for l in range(DEPTH):
        mod = (jax.nn.silu(c) @ w_mod[l] + b_mod[l])[:, None, :]
        shift_a, scale_a, gate_a, shift_m, scale_m, gate_m = jnp.split(mod, N_MOD, axis=-1)

        u = x * (1.0 + scale_a) + shift_a
        h = u @ w_in[l]
        qa, ka, va, qb, kb, vb = jnp.split(h, split_at, axis=-1)
        hd_a = (b, s, N_HEADS_A, HEAD_DIM_A)
        oa = dilated_attention(qa.reshape(hd_a), ka.reshape(hd_a), va.reshape(hd_a), slopes_a)
        ob, _ = banded_attention(qb.reshape(b, s, N_KV_HEADS_B, GQA_GROUP, HEAD_DIM_B),
                                 kb.reshape(b, s, N_KV_HEADS_B, HEAD_DIM_B),
                                 vb.reshape(b, s, N_KV_HEADS_B, HEAD_DIM_B),
                                 slopes_b, WINDOW_B - 1, 1,
                                 sinks[l].reshape(N_KV_HEADS_B, GQA_GROUP))
        mixed = jnp.concatenate([oa.reshape(b, s, WIDTH_A), ob.reshape(b, s, WIDTH_B)], axis=-1)
        att = mixed @ w_out[l]
        x = layer_norm(DEEPNORM_ALPHA * x + (1.0 + gate_a) * att, ln1_g[l], ln1_b[l])

        u = x * (1.0 + scale_m) + shift_m
        hup = causal_depthwise_conv(u @ w_up[l], conv_w[l], conv_b[l])
        g, val = jnp.split(hup, 2, axis=-1)
        y = (jax.nn.silu(g) * val) @ w_down[l]
        x = layer_norm(DEEPNORM_ALPHA * x + (1.0 + gate_m) * y, ln2_g[l], ln2_b[l])
    return x
```

```python
import functools
import math

import jax
import jax.numpy as jnp
from jax import lax
from jax.experimental import pallas as pl
from jax.experimental.pallas import tpu as pltpu

F32 = jnp.float32
BF16 = jnp.bfloat16

D_MODEL = 4096
SEQ = 8192
DEPTH = 2
HEAD_DIM_A = 128
N_HEADS_A = 16
DILATIONS = (1, 4, 16)
MAX_DIST_A = 128
HEAD_DIM_B = 64
N_Q_HEADS_B = 32
N_KV_HEADS_B = 4
GQA_GROUP = 8
MAX_DIST_B = 127
WIDTH_A = N_HEADS_A * HEAD_DIM_A
WIDTH_B = N_Q_HEADS_B * HEAD_DIM_B
KV_WIDTH_B = N_KV_HEADS_B * HEAD_DIM_B
QKV_A = 3 * WIDTH_A
N_ALIBI_HEADS = N_HEADS_A + N_Q_HEADS_B
D_FF = 11008
CONV_WIDTH = 3
N_MOD = 6
DEEPNORM_ALPHA = (2 * DEPTH) ** 0.25
LN_EPS = 1e-5
BLOCK = 128
NEG = -1e30

SPAN = 2048
MM_TM = 1024
MM_TN = 1024
DIL_TN = 512
MLP_TM = 512
MLP_TN = 256
LN_TM = 256
MOD_TN = 512
MIB = 1024 * 1024


def _params(sem, vmem_mib):
    return pltpu.CompilerParams(dimension_semantics=sem, vmem_limit_bytes=vmem_mib * MIB)


def _mod_kernel(c_ref, w_ref, b_ref, o_ref, cb_ref):
    @pl.when((pl.program_id(0) == 0) & (pl.program_id(1) == 0))
    def _():
        cc = c_ref[...]
        cb_ref[...] = jnp.broadcast_to(cc * jax.nn.sigmoid(cc), cb_ref.shape)

    k_dim = w_ref.shape[1]
    n_grp = w_ref.shape[2] // 128

    def body(k, accs):
        r = pl.multiple_of(k * 8, 8)
        cb = cb_ref[pl.ds(r, 8), :]
        return tuple(a + w_ref[0, pl.ds(r, 8), g * 128:(g + 1) * 128] * cb
                     for g, a in enumerate(accs))

    accs = lax.fori_loop(0, k_dim // 8, body,
                         tuple(jnp.zeros((8, 128), F32) for _ in range(n_grp)), unroll=8)
    for g, a in enumerate(accs):
        o_ref[0, :, g * 128:(g + 1) * 128] = (jnp.sum(a, axis=0, keepdims=True)
                                              + b_ref[0, :, g * 128:(g + 1) * 128])


def _mod_all(c, w_mod, b_mod):
    depth, d, n = w_mod.shape
    return pl.pallas_call(
        _mod_kernel,
        grid=(depth, n // MOD_TN),
        in_specs=[pl.BlockSpec((d, 1), lambda l, j: (0, 0)),
                  pl.BlockSpec((1, d, MOD_TN), lambda l, j: (l, 0, j)),
                  pl.BlockSpec((1, 1, MOD_TN), lambda l, j: (l, 0, j))],
        out_specs=pl.BlockSpec((1, 1, MOD_TN), lambda l, j: (l, 0, j)),
        out_shape=jax.ShapeDtypeStruct((depth, 1, n), F32),
        scratch_shapes=[pltpu.VMEM((d, 128), F32)],
        compiler_params=_params(("arbitrary", "arbitrary"), 40),
        name="mod_matvec",
    )(c.reshape(d, 1), w_mod, b_mod.reshape(depth, 1, n))


def _modulate_kernel(x_ref, sc_ref, sh_ref, u_ref):
    u_ref[...] = (x_ref[...] * (1.0 + sc_ref[...]) + sh_ref[...]).astype(u_ref.dtype)


def _modulate(x, scale, shift):
    s, d = x.shape
    row = pl.BlockSpec((1, d), lambda i: (0, 0))
    return pl.pallas_call(
        _modulate_kernel,
        grid=(s // LN_TM,),
        in_specs=[pl.BlockSpec((LN_TM, d), lambda i: (i, 0)), row, row],
        out_specs=pl.BlockSpec((LN_TM, d), lambda i: (i, 0)),
        out_shape=jax.ShapeDtypeStruct((s, d), BF16),
        compiler_params=_params(("arbitrary",), 32),
        name="modulate",
    )(x, scale, shift)


def _ln_kernel(x_ref, y_ref, gate_ref, g_ref, b_ref, *rest, with_next):
    z = DEEPNORM_ALPHA * x_ref[...] + (1.0 + gate_ref[...]) * y_ref[...]
    mu = jnp.mean(z, axis=-1, keepdims=True)
    zc = z - mu
    var = jnp.mean(zc * zc, axis=-1, keepdims=True)
    xn = zc * lax.rsqrt(var + LN_EPS) * g_ref[...] + b_ref[...]
    if with_next:
        sc_ref, sh_ref, xo_ref, uo_ref = rest
        xo_ref[...] = xn
        uo_ref[...] = (xn * (1.0 + sc_ref[...]) + sh_ref[...]).astype(uo_ref.dtype)
    else:
        (xo_ref,) = rest
        xo_ref[...] = xn


def _residual_ln(x, y, gate, g, b, nxt=None):
    s, d = x.shape
    row = pl.BlockSpec((1, d), lambda i: (0, 0))
    tile = pl.BlockSpec((LN_TM, d), lambda i: (i, 0))
    ins = [x, y, gate, g, b]
    in_specs = [tile, tile, row, row, row]
    out_shape = [jax.ShapeDtypeStruct((s, d), F32)]
    out_specs = [tile]
    if nxt is not None:
        ins += list(nxt)
        in_specs += [row, row]
        out_shape.append(jax.ShapeDtypeStruct((s, d), BF16))
        out_specs.append(tile)
    res = pl.pallas_call(
        functools.partial(_ln_kernel, with_next=nxt is not None),
        grid=(s // LN_TM,),
        in_specs=in_specs, out_specs=out_specs, out_shape=out_shape,
        compiler_params=_params(("arbitrary",), 48),
        name="residual_ln",
    )(*ins)
    return res if nxt is not None else (res[0], None)


def _mm_kernel(*refs, n_lhs, dilated):
    lhs = refs[:n_lhs]
    rhs = refs[n_lhs:2 * n_lhs]
    rest = refs[2 * n_lhs:]
    acc = jnp.dot(lhs[0][...], rhs[0][...], preferred_element_type=F32)
    for a, w in zip(lhs[1:], rhs[1:]):
        acc = acc + jnp.dot(a[...], w[...], preferred_element_type=F32)
    if not dilated:
        (o_ref,) = rest
        o_ref[...] = acc.astype(o_ref.dtype)
        return
    o1_ref, o4_ref, o16_ref, acc_ref = rest
    o1_ref[...] = acc.astype(o1_ref.dtype)
    n_cb, tm, _ = acc_ref.shape
    for cb in range(n_cb):
        acc_ref[cb] = acc[:, cb * 128:(cb + 1) * 128]
    for o_ref, d in ((o4_ref, DILATIONS[1]), (o16_ref, DILATIONS[2])):
        for r in range(d):
            for cb in range(n_cb):
                o_ref[r, :, cb * 128:(cb + 1) * 128] = (
                    acc_ref[cb, pl.ds(r, tm // d, stride=d), :].astype(o_ref.dtype))


def _in_proj_dilated(u, w, n_cols):
    s, k = u.shape
    d4, d16 = DILATIONS[1], DILATIONS[2]
    return pl.pallas_call(
        functools.partial(_mm_kernel, n_lhs=1, dilated=True),
        grid=(s // MM_TM, n_cols // DIL_TN),
        in_specs=[pl.BlockSpec((MM_TM, k), lambda i, j: (i, 0)),
                  pl.BlockSpec((k, DIL_TN), lambda i, j: (0, j))],
        out_specs=[pl.BlockSpec((MM_TM, DIL_TN), lambda i, j: (i, j)),
                   pl.BlockSpec((d4, MM_TM // d4, DIL_TN), lambda i, j: (0, i, j)),
                   pl.BlockSpec((d16, MM_TM // d16, DIL_TN), lambda i, j: (0, i, j))],
        out_shape=[jax.ShapeDtypeStruct((s, n_cols), BF16),
                   jax.ShapeDtypeStruct((d4, s // d4, n_cols), BF16),
                   jax.ShapeDtypeStruct((d16, s // d16, n_cols), BF16)],
        scratch_shapes=[pltpu.VMEM((DIL_TN // 128, MM_TM, 128), F32)],
        compiler_params=_params(("arbitrary", "arbitrary"), 56),
        name="in_proj_dilated",
    )(u, w)


def _matmul(lhs, rhs, rhs_blocks, n_cols, out_dtype, name):
    s = lhs[0].shape[0]
    in_specs = [pl.BlockSpec((MM_TM, a.shape[1]), lambda i, j: (i, 0)) for a in lhs]
    for a, (rb, cb) in zip(lhs, rhs_blocks):
        in_specs.append(pl.BlockSpec((a.shape[1], MM_TN), lambda i, j, rb=rb, cb=cb: (rb, cb + j)))
    return pl.pallas_call(
        functools.partial(_mm_kernel, n_lhs=len(lhs), dilated=False),
        grid=(s // MM_TM, n_cols // MM_TN),
        in_specs=in_specs,
        out_specs=pl.BlockSpec((MM_TM, MM_TN), lambda i, j: (i, j)),
        out_shape=jax.ShapeDtypeStruct((s, n_cols), out_dtype),
        compiler_params=_params(("arbitrary", "arbitrary"), 56),
        name=name,
    )(*lhs, *rhs)


def _band_bias(slope, dist_scale, max_dist):
    qi = lax.broadcasted_iota(jnp.int32, (BLOCK, 2 * BLOCK), 0)
    kj = lax.broadcasted_iota(jnp.int32, (BLOCK, 2 * BLOCK), 1)
    dist = qi + BLOCK - kj
    band = (dist >= 0) & (dist <= max_dist)
    bias = jnp.where(band, -(slope * dist_scale) * dist.astype(F32), NEG)
    return bias, jnp.where(kj >= BLOCK, bias, NEG)


def _scores(q, kk, scale, bias):
    s = lax.dot_general(q, kk, (((1,), (1,)), ((), ())), preferred_element_type=F32)
    return s * scale + bias


def _attn_a_kernel(slope_ref,
                   q1_ref, k1_ref, v1_ref, kp1_ref, vp1_ref,
                   q4_ref, k4_ref, v4_ref, kp4_ref, vp4_ref,
                   q16_ref, k16_ref, v16_ref, kp16_ref, vp16_ref,
                   o_ref,
                   kb1, vb1, kb4, vb4, kb16, vb16, bias_ref, acc_ref, lse_ref, nat_ref, lsen_ref):
    head = pl.program_id(0)
    sp = pl.program_id(1)
    slope = slope_ref[head]
    scale = HEAD_DIM_A ** -0.5
    d4, d16 = DILATIONS[1], DILATIONS[2]
    n4 = SPAN // d4

    for bi, d in enumerate(DILATIONS):
        b0, b1 = _band_bias(slope, float(d), MAX_DIST_A)
        bias_ref[2 * bi] = b0
        bias_ref[2 * bi + 1] = b1

    kb1[0:BLOCK] = kp1_ref[...]
    kb1[BLOCK:] = k1_ref[...]
    vb1[0:BLOCK] = vp1_ref[...]
    vb1[BLOCK:] = v1_ref[...]
    kb4[:, 0:BLOCK] = kp4_ref[...]
    kb4[:, BLOCK:] = k4_ref[...]
    vb4[:, 0:BLOCK] = vp4_ref[...]
    vb4[:, BLOCK:] = v4_ref[...]
    kb16[:, 0:BLOCK] = kp16_ref[...]
    kb16[:, BLOCK:] = k16_ref[...]
    vb16[:, 0:BLOCK] = vp16_ref[...]
    vb16[:, BLOCK:] = v16_ref[...]

    def block(q, kk, vv, bias):
        s = _scores(q, kk, scale, bias)
        m = jnp.max(s, axis=-1, keepdims=True)
        p = jnp.exp(s - m)
        l = jnp.sum(p, axis=-1, keepdims=True)
        o = jnp.dot(p.astype(BF16), vv, preferred_element_type=F32) * (1.0 / l)
        return o, jnp.broadcast_to(m + jnp.log(l), o.shape)

    def body1(b, carry):
        r0 = pl.multiple_of(b * BLOCK, BLOCK)
        first = ((sp == 0) & (b == 0)).astype(jnp.int32)
        o, lse = block(q1_ref[pl.ds(r0, BLOCK), :], kb1[pl.ds(r0, 2 * BLOCK), :],
                       vb1[pl.ds(r0, 2 * BLOCK), :], bias_ref[first])
        acc_ref[0, pl.ds(r0, BLOCK), :] = o
        lse_ref[0, pl.ds(r0, BLOCK), :] = lse
        return carry

    lax.fori_loop(0, SPAN // BLOCK, body1, 0)

    def body4(idx, carry):
        r = idx // (n4 // BLOCK)
        b = idx % (n4 // BLOCK)
        r0 = pl.multiple_of(b * BLOCK, BLOCK)
        first = ((sp == 0) & (b == 0)).astype(jnp.int32)
        o, lse = block(q4_ref[r, pl.ds(r0, BLOCK), :], kb4[r, pl.ds(r0, 2 * BLOCK), :],
                       vb4[r, pl.ds(r0, 2 * BLOCK), :], bias_ref[2 + first])
        w0 = pl.multiple_of(idx * BLOCK, BLOCK)
        acc_ref[1, pl.ds(w0, BLOCK), :] = o
        lse_ref[1, pl.ds(w0, BLOCK), :] = lse
        return carry

    lax.fori_loop(0, SPAN // BLOCK, body4, 0)

    def body16(r, carry):
        first = (sp == 0).astype(jnp.int32)
        o, lse = block(q16_ref[r], kb16[r], vb16[r], bias_ref[4 + first])
        w0 = pl.multiple_of(r * BLOCK, BLOCK)
        acc_ref[2, pl.ds(w0, BLOCK), :] = o
        lse_ref[2, pl.ds(w0, BLOCK), :] = lse
        return carry

    lax.fori_loop(0, d16, body16, 0)

    for r in range(d4):
        nat_ref[0, pl.ds(r, n4, stride=d4), :] = acc_ref[1, r * n4:(r + 1) * n4, :]
        lsen_ref[0, pl.ds(r, n4, stride=d4), :] = lse_ref[1, r * n4:(r + 1) * n4, :]
    for r in range(d16):
        nat_ref[1, pl.ds(r, BLOCK, stride=d16), :] = acc_ref[2, r * BLOCK:(r + 1) * BLOCK, :]
        lsen_ref[1, pl.ds(r, BLOCK, stride=d16), :] = lse_ref[2, r * BLOCK:(r + 1) * BLOCK, :]

    chunk = 256

    def merge(c, carry):
        r0 = pl.multiple_of(c * chunk, chunk)
        rows = pl.ds(r0, chunk)
        l1 = lse_ref[0, rows, :]
        l4 = lsen_ref[0, rows, :]
        l16 = lsen_ref[1, rows, :]
        mx = jnp.maximum(jnp.maximum(l1, l4), l16)
        e1 = jnp.exp(l1 - mx)
        e4 = jnp.exp(l4 - mx)
        e16 = jnp.exp(l16 - mx)
        inv = 1.0 / (e1 + e4 + e16)
        o = ((e1 * inv) * acc_ref[0, rows, :] + (e4 * inv) * nat_ref[0, rows, :]
             + (e16 * inv) * nat_ref[1, rows, :])
        o_ref[rows, :] = o.astype(o_ref.dtype)
        return carry

    lax.fori_loop(0, SPAN // chunk, merge, 0)


def _attn_a(h1, h4, h16, slopes_a):
    s = h1.shape[0]
    d4, d16 = DILATIONS[1], DILATIONS[2]
    n4 = SPAN // d4
    nh = N_HEADS_A
    blk1 = SPAN // BLOCK

    def cur1(off):
        return pl.BlockSpec((SPAN, HEAD_DIM_A), lambda h, sp: (sp, off + h))

    def prev1(off):
        return pl.BlockSpec((BLOCK, HEAD_DIM_A), lambda h, sp: (jnp.maximum(sp * blk1 - 1, 0), off + h))

    def cur4(off):
        return pl.BlockSpec((d4, n4, HEAD_DIM_A), lambda h, sp: (0, sp, off + h))

    def prev4(off):
        return pl.BlockSpec((d4, BLOCK, HEAD_DIM_A),
                            lambda h, sp: (0, jnp.maximum(sp * (n4 // BLOCK) - 1, 0), off + h))

    def cur16(off):
        return pl.BlockSpec((d16, BLOCK, HEAD_DIM_A), lambda h, sp: (0, sp, off + h))

    def prev16(off):
        return pl.BlockSpec((d16, BLOCK, HEAD_DIM_A), lambda h, sp: (0, jnp.maximum(sp - 1, 0), off + h))

    in_specs = [pl.BlockSpec(memory_space=pltpu.SMEM)]
    args = [slopes_a]
    for arr, cur, prev in ((h1, cur1, prev1), (h4, cur4, prev4), (h16, cur16, prev16)):
        in_specs += [cur(0), cur(nh), cur(2 * nh), prev(nh), prev(2 * nh)]
        args += [arr] * 5
    return pl.pallas_call(
        _attn_a_kernel,
        grid=(nh, s // SPAN),
        in_specs=in_specs,
        out_specs=pl.BlockSpec((SPAN, HEAD_DIM_A), lambda h, sp: (sp, h)),
        out_shape=jax.ShapeDtypeStruct((s, WIDTH_A), BF16),
        scratch_shapes=[
            pltpu.VMEM((BLOCK + SPAN, HEAD_DIM_A), BF16), pltpu.VMEM((BLOCK + SPAN, HEAD_DIM_A), BF16),
            pltpu.VMEM((d4, BLOCK + n4, HEAD_DIM_A), BF16), pltpu.VMEM((d4, BLOCK + n4, HEAD_DIM_A), BF16),
            pltpu.VMEM((d16, 2 * BLOCK, HEAD_DIM_A), BF16), pltpu.VMEM((d16, 2 * BLOCK, HEAD_DIM_A), BF16),
            pltpu.VMEM((6, BLOCK, 2 * BLOCK), F32),
            pltpu.VMEM((3, SPAN, HEAD_DIM_A), F32),
            pltpu.VMEM((3, SPAN, HEAD_DIM_A), F32),
            pltpu.VMEM((2, SPAN, HEAD_DIM_A), F32),
            pltpu.VMEM((2, SPAN, HEAD_DIM_A), F32),
        ],
        compiler_params=_params(("arbitrary", "arbitrary"), 48),
        name="attn_dilated",
    )(*args)


def _attn_b_kernel(slope_ref, sink_ref, q_ref, k_ref, v_ref, kp_ref, vp_ref, o_ref, kb, vb, bias_ref):
    hk = pl.program_id(0)
    sp = pl.program_id(1)
    scale = HEAD_DIM_B ** -0.5

    for g in range(GQA_GROUP):
        b0, b1 = _band_bias(slope_ref[hk * GQA_GROUP + g], 1.0, MAX_DIST_B)
        bias_ref[2 * g] = b0
        bias_ref[2 * g + 1] = b1

    kb[0:BLOCK] = kp_ref[...]
    kb[BLOCK:] = k_ref[...]
    vb[0:BLOCK] = vp_ref[...]
    vb[BLOCK:] = v_ref[...]

    lane = lax.broadcasted_iota(jnp.int32, (BLOCK, 2 * HEAD_DIM_B), 1)
    low = lane < HEAD_DIM_B

    def body(b, carry):
        r0 = pl.multiple_of(b * BLOCK, BLOCK)
        first = ((sp == 0) & (b == 0)).astype(jnp.int32)
        kk = kb[pl.ds(r0, 2 * BLOCK), :]
        vv = vb[pl.ds(r0, 2 * BLOCK), :]
        for pr in range(GQA_GROUP // 2):
            q2 = q_ref[pl.ds(r0, BLOCK), pr * 128:(pr + 1) * 128]
            outs = []
            for half in range(2):
                g = 2 * pr + half
                keep = low if half == 0 else jnp.logical_not(low)
                qq = jnp.where(keep, q2, jnp.zeros_like(q2))
                s = _scores(qq, kk, scale, bias_ref[2 * g + first])
                sink = sink_ref[hk * GQA_GROUP + g]
                m = jnp.maximum(jnp.max(s, axis=-1, keepdims=True), sink)
                p = jnp.exp(s - m)
                l = jnp.sum(p, axis=-1, keepdims=True) + jnp.exp(sink - m)
                outs.append(jnp.dot(p.astype(BF16), vv, preferred_element_type=F32) * (1.0 / l))
            o_ref[pl.ds(r0, BLOCK), pr * 128:(pr + 1) * 128] = jnp.where(low, outs[0], outs[1]).astype(o_ref.dtype)
        return carry

    lax.fori_loop(0, SPAN // BLOCK, body, 0)


def _attn_b(qb, kv2, slopes_b, sinks):
    s = qb.shape[0]
    gw = GQA_GROUP * HEAD_DIM_B
    blk = SPAN // BLOCK
    nk = N_KV_HEADS_B

    def cur(off):
        return pl.BlockSpec((SPAN, 128), lambda h, sp: (sp, off + h))

    def prev(off):
        return pl.BlockSpec((BLOCK, 128), lambda h, sp: (jnp.maximum(sp * blk - 1, 0), off + h))

    smem = pl.BlockSpec(memory_space=pltpu.SMEM)
    return pl.pallas_call(
        _attn_b_kernel,
        grid=(nk, s // SPAN),
        in_specs=[smem, smem, pl.BlockSpec((SPAN, gw), lambda h, sp: (sp, h)),
                  cur(0), cur(nk), prev(0), prev(nk)],
        out_specs=pl.BlockSpec((SPAN, gw), lambda h, sp: (sp, h)),
        out_shape=jax.ShapeDtypeStruct((s, WIDTH_B), BF16),
        scratch_shapes=[pltpu.VMEM((BLOCK + SPAN, 128), BF16), pltpu.VMEM((BLOCK + SPAN, 128), BF16),
                        pltpu.VMEM((2 * GQA_GROUP, BLOCK, 2 * BLOCK), F32)],
        compiler_params=_params(("arbitrary", "arbitrary"), 48),
        name="attn_swa",
    )(slopes_b, sinks, qb, kv2, kv2, kv2, kv2)


def _mlp_kernel(u_ref, wg_ref, wv_ref, cwg_ref, cwv_ref, cbg_ref, cbv_ref, wd_ref, o_ref, hbuf, carry):
    i = pl.program_id(0)
    j = pl.program_id(1)
    tm = u_ref.shape[0]
    tn = wg_ref.shape[1]
    u = u_ref[...]

    @pl.when(i == 0)
    def _():
        hbuf[0:8, :] = jnp.zeros((8, 2 * tn), F32)

    @pl.when(i > 0)
    def _():
        hbuf[0:8, :] = carry[j]

    hbuf[8:, 0:tn] = jnp.dot(u, wg_ref[...], preferred_element_type=F32)
    hbuf[8:, tn:] = jnp.dot(u, wv_ref[...], preferred_element_type=F32)
    carry[j] = hbuf[tm:tm + 8, :]

    def conv(c0, c1, cw_ref, cb_ref):
        acc = cw_ref[0:1, :] * hbuf[8 - 2:8 - 2 + tm, c0:c1]
        acc = acc + cw_ref[1:2, :] * hbuf[8 - 1:8 - 1 + tm, c0:c1]
        acc = acc + cw_ref[2:3, :] * hbuf[8:8 + tm, c0:c1]
        return acc + cb_ref[...]

    gate = conv(0, tn, cwg_ref, cbg_ref)
    val = conv(tn, 2 * tn, cwv_ref, cbv_ref)
    act = (gate * jax.nn.sigmoid(gate) * val).astype(BF16)
    contrib = jnp.dot(act, wd_ref[...], preferred_element_type=F32)

    @pl.when(j == 0)
    def _():
        o_ref[...] = contrib

    @pl.when(j > 0)
    def _():
        o_ref[...] += contrib


def _conv_ffn(u, w_up, conv_w, conv_b, w_down):
    s, d = u.shape
    nj = D_FF // MLP_TN
    return pl.pallas_call(
        _mlp_kernel,
        grid=(s // MLP_TM, nj),
        in_specs=[pl.BlockSpec((MLP_TM, d), lambda i, j: (i, 0)),
                  pl.BlockSpec((d, MLP_TN), lambda i, j: (0, j)),
                  pl.BlockSpec((d, MLP_TN), lambda i, j: (0, nj + j)),
                  pl.BlockSpec((CONV_WIDTH, MLP_TN), lambda i, j: (0, j)),
                  pl.BlockSpec((CONV_WIDTH, MLP_TN), lambda i, j: (0, nj + j)),
                  pl.BlockSpec((1, MLP_TN), lambda i, j: (0, j)),
                  pl.BlockSpec((1, MLP_TN), lambda i, j: (0, nj + j)),
                  pl.BlockSpec((MLP_TN, d), lambda i, j: (j, 0))],
        out_specs=pl.BlockSpec((MLP_TM, d), lambda i, j: (i, 0)),
        out_shape=jax.ShapeDtypeStruct((s, d), F32),
        scratch_shapes=[pltpu.VMEM((8 + MLP_TM, 2 * MLP_TN), F32),
                        pltpu.VMEM((nj, 8, 2 * MLP_TN), F32)],
        compiler_params=_params(("arbitrary", "arbitrary"), 56),
        name="conv_ffn",
    )(u, w_up, w_up, conv_w, conv_w, conv_b, conv_b, w_down)


def _alibi_slopes():
    i = jnp.arange(1, N_ALIBI_HEADS + 1, dtype=F32)
    return jnp.exp2(-8.0 * i / N_ALIBI_HEADS)


def _dup_heads(w):
    k = w.shape[0]
    w = w.reshape(k, N_KV_HEADS_B, 1, HEAD_DIM_B)
    return jnp.broadcast_to(w, (k, N_KV_HEADS_B, 2, HEAD_DIM_B)).reshape(k, 2 * KV_WIDTH_B)


def kernel(x, c, w_mod, b_mod, w_in, sinks, w_out, ln1_g, ln1_b, w_up, conv_w, conv_b, w_down, ln2_g, ln2_b):
    b, s, d = x.shape
    assert (b, s, d) == (1, SEQ, D_MODEL)
    slopes = _alibi_slopes()
    slopes_b, slopes_a = slopes[:N_Q_HEADS_B], slopes[N_Q_HEADS_B:]
    mod = _mod_all(c, w_mod, b_mod)
    x = x.reshape(s, d)
    u = None
    kv_off = QKV_A + WIDTH_B
    for l in range(DEPTH):
        shift_a, scale_a, gate_a, shift_m, scale_m, gate_m = (
            mod[l, :, n * d:(n + 1) * d] for n in range(N_MOD))
        w_in_l = w_in[l].astype(BF16)
        w_kv2 = jnp.concatenate([_dup_heads(w_in[l][:, kv_off:kv_off + KV_WIDTH_B]),
                                 _dup_heads(w_in[l][:, kv_off + KV_WIDTH_B:])], axis=1).astype(BF16)
        w_out_l = w_out[l].astype(BF16)
        w_up_l = w_up[l].astype(BF16)
        w_down_l = w_down[l].astype(BF16)

        if u is None:
            u = _modulate(x, scale_a, shift_a)
        h1, h4, h16 = _in_proj_dilated(u, w_in_l, QKV_A)
        qb = _matmul([u], [w_in_l], [(0, QKV_A // MM_TN)], WIDTH_B, BF16, "in_proj_qb")
        kv2 = _matmul([u], [w_kv2], [(0, 0)], 4 * KV_WIDTH_B, BF16, "in_proj_kvb")
        mixed_a = _attn_a(h1, h4, h16, slopes_a)
        mixed_b = _attn_b(qb, kv2, slopes_b, sinks[l])
        att = _matmul([mixed_a, mixed_b], [w_out_l, w_out_l], [(0, 0), (1, 0)], d, F32, "out_proj")
        x, u = _residual_ln(x, att, gate_a, ln1_g[l].reshape(1, d), ln1_b[l].reshape(1, d),
                            (scale_m, shift_m))
        y = _conv_ffn(u, w_up_l, conv_w[l], conv_b[l].reshape(1, 2 * D_FF), w_down_l)
        nxt = None
        if l + 1 < DEPTH:
            nxt = (mod[l + 1, :, d:2 * d], mod[l + 1, :, 0:d])
        x, u = _residual_ln(x, y, gate_m, ln2_g[l].reshape(1, d), ln2_b[l].reshape(1, d), nxt)
    return x.reshape(b, s, d)
```

```python
import functools
import math

import jax
import jax.numpy as jnp
from jax import lax
from jax.experimental import pallas as pl
from jax.experimental.pallas import tpu as pltpu

F32 = jnp.float32
BF16 = jnp.bfloat16

D_MODEL = 4096
SEQ = 8192
DEPTH = 2
HEAD_DIM_A = 128
N_HEADS_A = 16
DILATIONS = (1, 4, 16)
MAX_DIST_A = 128
HEAD_DIM_B = 64
N_Q_HEADS_B = 32
N_KV_HEADS_B = 4
GQA_GROUP = 8
MAX_DIST_B = 127
WIDTH_A = N_HEADS_A * HEAD_DIM_A
WIDTH_B = N_Q_HEADS_B * HEAD_DIM_B
KV_WIDTH_B = N_KV_HEADS_B * HEAD_DIM_B
QKV_A = 3 * WIDTH_A
N_ALIBI_HEADS = N_HEADS_A + N_Q_HEADS_B
D_FF = 11008
CONV_WIDTH = 3
N_MOD = 6
DEEPNORM_ALPHA = (2 * DEPTH) ** 0.25
LN_EPS = 1e-5
BLOCK = 128
NEG = -1e30

SPAN = 2048
MM_TM = 1024
MM_TN = 512
MLP_TM = 512
MLP_TN = 512
FF_PAD = MLP_TN * math.ceil(D_FF / MLP_TN)
LN_TM = 256
ATTN_UNROLL = 8
MOD_TN = 512
MIB = 1024 * 1024


def _params(sem, vmem_mib):
    return pltpu.CompilerParams(dimension_semantics=sem, vmem_limit_bytes=vmem_mib * MIB)


def _mod_kernel(c_ref, w_ref, b_ref, o_ref, cb_ref):
    @pl.when((pl.program_id(0) == 0) & (pl.program_id(1) == 0))
    def _():
        cc = c_ref[...]
        cb_ref[...] = jnp.broadcast_to(cc * jax.nn.sigmoid(cc), cb_ref.shape)

    k_dim = w_ref.shape[1]
    n_grp = w_ref.shape[2] // 128

    def body(k, accs):
        r = pl.multiple_of(k * 8, 8)
        cb = cb_ref[pl.ds(r, 8), :]
        return tuple(a + w_ref[0, pl.ds(r, 8), g * 128:(g + 1) * 128] * cb
                     for g, a in enumerate(accs))

    accs = lax.fori_loop(0, k_dim // 8, body,
                         tuple(jnp.zeros((8, 128), F32) for _ in range(n_grp)), unroll=8)
    for g, a in enumerate(accs):
        o_ref[0, :, g * 128:(g + 1) * 128] = (jnp.sum(a, axis=0, keepdims=True)
                                              + b_ref[0, :, g * 128:(g + 1) * 128])


def _mod_all(c, w_mod, b_mod):
    depth, d, n = w_mod.shape
    return pl.pallas_call(
        _mod_kernel,
        grid=(depth, n // MOD_TN),
        in_specs=[pl.BlockSpec((d, 1), lambda l, j: (0, 0)),
                  pl.BlockSpec((1, d, MOD_TN), lambda l, j: (l, 0, j)),
                  pl.BlockSpec((1, 1, MOD_TN), lambda l, j: (l, 0, j))],
        out_specs=pl.BlockSpec((1, 1, MOD_TN), lambda l, j: (l, 0, j)),
        out_shape=jax.ShapeDtypeStruct((depth, 1, n), F32),
        scratch_shapes=[pltpu.VMEM((d, 128), F32)],
        compiler_params=_params(("arbitrary", "arbitrary"), 40),
        name="mod_matvec",
    )(c.reshape(d, 1), w_mod, b_mod.reshape(depth, 1, n))


def _modulate_kernel(x_ref, sc_ref, sh_ref, u_ref):
    u_ref[...] = (x_ref[...] * (1.0 + sc_ref[...]) + sh_ref[...]).astype(u_ref.dtype)


def _modulate(x, scale, shift):
    s, d = x.shape
    row = pl.BlockSpec((1, d), lambda i: (0, 0))
    return pl.pallas_call(
        _modulate_kernel,
        grid=(s // LN_TM,),
        in_specs=[pl.BlockSpec((LN_TM, d), lambda i: (i, 0)), row, row],
        out_specs=pl.BlockSpec((LN_TM, d), lambda i: (i, 0)),
        out_shape=jax.ShapeDtypeStruct((s, d), BF16),
        compiler_params=_params(("arbitrary",), 32),
        name="modulate",
    )(x, scale, shift)


def _ln_kernel(x_ref, y_ref, gate_ref, g_ref, b_ref, *rest, with_next):
    z = DEEPNORM_ALPHA * x_ref[...] + (1.0 + gate_ref[...]) * y_ref[...]
    mu = jnp.mean(z, axis=-1, keepdims=True)
    zc = z - mu
    var = jnp.mean(zc * zc, axis=-1, keepdims=True)
    xn = zc * lax.rsqrt(var + LN_EPS) * g_ref[...] + b_ref[...]
    if with_next:
        sc_ref, sh_ref, xo_ref, uo_ref = rest
        xo_ref[...] = xn
        uo_ref[...] = (xn * (1.0 + sc_ref[...]) + sh_ref[...]).astype(uo_ref.dtype)
    else:
        (xo_ref,) = rest
        xo_ref[...] = xn


def _residual_ln(x, y, gate, g, b, nxt=None):
    s, d = x.shape
    row = pl.BlockSpec((1, d), lambda i: (0, 0))
    tile = pl.BlockSpec((LN_TM, d), lambda i: (i, 0))
    ins = [x, y, gate, g, b]
    in_specs = [tile, tile, row, row, row]
    out_shape = [jax.ShapeDtypeStruct((s, d), F32)]
    out_specs = [tile]
    if nxt is not None:
        ins += list(nxt)
        in_specs += [row, row]
        out_shape.append(jax.ShapeDtypeStruct((s, d), BF16))
        out_specs.append(tile)
    res = pl.pallas_call(
        functools.partial(_ln_kernel, with_next=nxt is not None),
        grid=(s // LN_TM,),
        in_specs=in_specs, out_specs=out_specs, out_shape=out_shape,
        compiler_params=_params(("arbitrary",), 48),
        name="residual_ln",
    )(*ins)
    return res if nxt is not None else (res[0], None)


def _mm_kernel(*refs, n_lhs, dilated):
    lhs = refs[:n_lhs]
    w_ref = refs[n_lhs]
    if dilated:
        o1_ref, o4_ref, o16_ref, wb_ref, acc_ref = refs[n_lhs + 1:]
    else:
        o_ref, wb_ref = refs[n_lhs + 1:]

    @pl.when(pl.program_id(1) == 0)
    def _():
        wb_ref[...] = w_ref[...].astype(wb_ref.dtype)

    acc = None
    k0 = 0
    for a in lhs:
        kk = a.shape[1]
        part = jnp.dot(a[...], wb_ref[k0:k0 + kk, :], preferred_element_type=F32)
        acc = part if acc is None else acc + part
        k0 += kk
    if not dilated:
        o_ref[...] = acc.astype(o_ref.dtype)
        return
    o1_ref[...] = acc.astype(o1_ref.dtype)
    n_cb, tm, _ = acc_ref.shape
    for cb in range(n_cb):
        acc_ref[cb] = acc[:, cb * 128:(cb + 1) * 128]
    for o_ref, d in ((o4_ref, DILATIONS[1]), (o16_ref, DILATIONS[2])):
        for r in range(d):
            for cb in range(n_cb):
                o_ref[r, :, cb * 128:(cb + 1) * 128] = (
                    acc_ref[cb, pl.ds(r, tm // d, stride=d), :].astype(o_ref.dtype))


def _weight_spec(w, layer, col_block0):
    k = w.shape[-2]
    if w.ndim == 3:
        return pl.BlockSpec((None, k, MM_TN), lambda j, i: (layer, 0, col_block0 + j))
    return pl.BlockSpec((k, MM_TN), lambda j, i: (0, col_block0 + j))


def _in_proj_dilated(u, w, layer, n_cols):
    s, k = u.shape
    d4, d16 = DILATIONS[1], DILATIONS[2]
    return pl.pallas_call(
        functools.partial(_mm_kernel, n_lhs=1, dilated=True),
        grid=(n_cols // MM_TN, s // MM_TM),
        in_specs=[pl.BlockSpec((MM_TM, k), lambda j, i: (i, 0)), _weight_spec(w, layer, 0)],
        out_specs=[pl.BlockSpec((MM_TM, MM_TN), lambda j, i: (i, j)),
                   pl.BlockSpec((d4, MM_TM // d4, MM_TN), lambda j, i: (0, i, j)),
                   pl.BlockSpec((d16, MM_TM // d16, MM_TN), lambda j, i: (0, i, j))],
        out_shape=[jax.ShapeDtypeStruct((s, n_cols), BF16),
                   jax.ShapeDtypeStruct((d4, s // d4, n_cols), BF16),
                   jax.ShapeDtypeStruct((d16, s // d16, n_cols), BF16)],
        scratch_shapes=[pltpu.VMEM((k, MM_TN), BF16),
                        pltpu.VMEM((MM_TN // 128, MM_TM, 128), F32)],
        compiler_params=_params(("arbitrary", "arbitrary"), 56),
        name="in_proj_dilated",
    )(u, w)


def _matmul(lhs, w, layer, col_block0, n_cols, out_dtype, name):
    s = lhs[0].shape[0]
    k = w.shape[-2]
    assert sum(a.shape[1] for a in lhs) == k
    return pl.pallas_call(
        functools.partial(_mm_kernel, n_lhs=len(lhs), dilated=False),
        grid=(n_cols // MM_TN, s // MM_TM),
        in_specs=[pl.BlockSpec((MM_TM, a.shape[1]), lambda j, i: (i, 0)) for a in lhs]
        + [_weight_spec(w, layer, col_block0)],
        out_specs=pl.BlockSpec((MM_TM, MM_TN), lambda j, i: (i, j)),
        out_shape=jax.ShapeDtypeStruct((s, n_cols), out_dtype),
        scratch_shapes=[pltpu.VMEM((k, MM_TN), BF16)],
        compiler_params=_params(("arbitrary", "arbitrary"), 56),
        name=name,
    )(*lhs, w)


def _band_bias(slope, dist_scale, max_dist):
    qi = lax.broadcasted_iota(jnp.int32, (BLOCK, 2 * BLOCK), 0)
    kj = lax.broadcasted_iota(jnp.int32, (BLOCK, 2 * BLOCK), 1)
    dist = qi + BLOCK - kj
    band = (dist >= 0) & (dist <= max_dist)
    bias = jnp.where(band, -(slope * dist_scale) * dist.astype(F32), NEG)
    return bias, jnp.where(kj >= BLOCK, bias, NEG)


def _scores(q, kk, scale, bias):
    s = lax.dot_general(q, kk, (((1,), (1,)), ((), ())), preferred_element_type=F32)
    return s * scale + bias


def _attn_a_kernel(slope_ref,
                   q1_ref, k1_ref, v1_ref, kp1_ref, vp1_ref,
                   q4_ref, k4_ref, v4_ref, kp4_ref, vp4_ref,
                   q16_ref, k16_ref, v16_ref, kp16_ref, vp16_ref,
                   o_ref,
                   kb1, vb1, kb4, vb4, kb16, vb16, bias_ref, acc_ref, lse_ref, nat_ref, lsen_ref):
    head = pl.program_id(0)
    sp = pl.program_id(1)
    slope = slope_ref[head]
    scale = HEAD_DIM_A ** -0.5
    d4, d16 = DILATIONS[1], DILATIONS[2]
    n4 = SPAN // d4

    for bi, d in enumerate(DILATIONS):
        b0, b1 = _band_bias(slope, float(d), MAX_DIST_A)
        bias_ref[2 * bi] = b0
        bias_ref[2 * bi + 1] = b1

    kb1[0:BLOCK] = kp1_ref[...]
    kb1[BLOCK:] = k1_ref[...]
    vb1[0:BLOCK] = vp1_ref[...]
    vb1[BLOCK:] = v1_ref[...]
    kb4[:, 0:BLOCK] = kp4_ref[...]
    kb4[:, BLOCK:] = k4_ref[...]
    vb4[:, 0:BLOCK] = vp4_ref[...]
    vb4[:, BLOCK:] = v4_ref[...]
    kb16[:, 0:BLOCK] = kp16_ref[...]
    kb16[:, BLOCK:] = k16_ref[...]
    vb16[:, 0:BLOCK] = vp16_ref[...]
    vb16[:, BLOCK:] = v16_ref[...]

    def block(q, kk, vv, bias):
        s = _scores(q, kk, scale, bias)
        m = jnp.max(s, axis=-1, keepdims=True)
        p = jnp.exp(s - m)
        l = jnp.sum(p, axis=-1, keepdims=True)
        o = jnp.dot(p.astype(BF16), vv, preferred_element_type=F32) * (1.0 / l)
        return o, jnp.broadcast_to(m + jnp.log(l), o.shape)

    def body1(b, carry):
        r0 = pl.multiple_of(b * BLOCK, BLOCK)
        first = ((sp == 0) & (b == 0)).astype(jnp.int32)
        o, lse = block(q1_ref[pl.ds(r0, BLOCK), :], kb1[pl.ds(r0, 2 * BLOCK), :],
                       vb1[pl.ds(r0, 2 * BLOCK), :], bias_ref[first])
        acc_ref[0, pl.ds(r0, BLOCK), :] = o
        lse_ref[0, pl.ds(r0, BLOCK), :] = lse
        return carry

    lax.fori_loop(0, SPAN // BLOCK, body1, 0, unroll=ATTN_UNROLL)

    def body4(idx, carry):
        r = idx // (n4 // BLOCK)
        b = idx % (n4 // BLOCK)
        r0 = pl.multiple_of(b * BLOCK, BLOCK)
        first = ((sp == 0) & (b == 0)).astype(jnp.int32)
        o, lse = block(q4_ref[r, pl.ds(r0, BLOCK), :], kb4[r, pl.ds(r0, 2 * BLOCK), :],
                       vb4[r, pl.ds(r0, 2 * BLOCK), :], bias_ref[2 + first])
        w0 = pl.multiple_of(idx * BLOCK, BLOCK)
        acc_ref[1, pl.ds(w0, BLOCK), :] = o
        lse_ref[1, pl.ds(w0, BLOCK), :] = lse
        return carry

    lax.fori_loop(0, SPAN // BLOCK, body4, 0, unroll=ATTN_UNROLL)

    def body16(r, carry):
        first = (sp == 0).astype(jnp.int32)
        o, lse = block(q16_ref[r], kb16[r], vb16[r], bias_ref[4 + first])
        w0 = pl.multiple_of(r * BLOCK, BLOCK)
        acc_ref[2, pl.ds(w0, BLOCK), :] = o
        lse_ref[2, pl.ds(w0, BLOCK), :] = lse
        return carry

    lax.fori_loop(0, d16, body16, 0, unroll=ATTN_UNROLL)

    for r in range(d4):
        nat_ref[0, pl.ds(r, n4, stride=d4), :] = acc_ref[1, r * n4:(r + 1) * n4, :]
        lsen_ref[0, pl.ds(r, n4, stride=d4), :] = lse_ref[1, r * n4:(r + 1) * n4, :]
    for r in range(d16):
        nat_ref[1, pl.ds(r, BLOCK, stride=d16), :] = acc_ref[2, r * BLOCK:(r + 1) * BLOCK, :]
        lsen_ref[1, pl.ds(r, BLOCK, stride=d16), :] = lse_ref[2, r * BLOCK:(r + 1) * BLOCK, :]

    chunk = 256

    def merge(c, carry):
        r0 = pl.multiple_of(c * chunk, chunk)
        rows = pl.ds(r0, chunk)
        l1 = lse_ref[0, rows, :]
        l4 = lsen_ref[0, rows, :]
        l16 = lsen_ref[1, rows, :]
        mx = jnp.maximum(jnp.maximum(l1, l4), l16)
        e1 = jnp.exp(l1 - mx)
        e4 = jnp.exp(l4 - mx)
        e16 = jnp.exp(l16 - mx)
        inv = 1.0 / (e1 + e4 + e16)
        o = ((e1 * inv) * acc_ref[0, rows, :] + (e4 * inv) * nat_ref[0, rows, :]
             + (e16 * inv) * nat_ref[1, rows, :])
        o_ref[rows, :] = o.astype(o_ref.dtype)
        return carry

    lax.fori_loop(0, SPAN // chunk, merge, 0)


def _attn_a(h1, h4, h16, slopes_a):
    s = h1.shape[0]
    d4, d16 = DILATIONS[1], DILATIONS[2]
    n4 = SPAN // d4
    nh = N_HEADS_A
    blk1 = SPAN // BLOCK

    def cur1(off):
        return pl.BlockSpec((SPAN, HEAD_DIM_A), lambda h, sp: (sp, off + h))

    def prev1(off):
        return pl.BlockSpec((BLOCK, HEAD_DIM_A), lambda h, sp: (jnp.maximum(sp * blk1 - 1, 0), off + h))

    def cur4(off):
        return pl.BlockSpec((d4, n4, HEAD_DIM_A), lambda h, sp: (0, sp, off + h))

    def prev4(off):
        return pl.BlockSpec((d4, BLOCK, HEAD_DIM_A),
                            lambda h, sp: (0, jnp.maximum(sp * (n4 // BLOCK) - 1, 0), off + h))

    def cur16(off):
        return pl.BlockSpec((d16, BLOCK, HEAD_DIM_A), lambda h, sp: (0, sp, off + h))

    def prev16(off):
        return pl.BlockSpec((d16, BLOCK, HEAD_DIM_A), lambda h, sp: (0, jnp.maximum(sp - 1, 0), off + h))

    in_specs = [pl.BlockSpec(memory_space=pltpu.SMEM)]
    args = [slopes_a]
    for arr, cur, prev in ((h1, cur1, prev1), (h4, cur4, prev4), (h16, cur16, prev16)):
        in_specs += [cur(0), cur(nh), cur(2 * nh), prev(nh), prev(2 * nh)]
        args += [arr] * 5
    return pl.pallas_call(
        _attn_a_kernel,
        grid=(nh, s // SPAN),
        in_specs=in_specs,
        out_specs=pl.BlockSpec((SPAN, HEAD_DIM_A), lambda h, sp: (sp, h)),
        out_shape=jax.ShapeDtypeStruct((s, WIDTH_A), BF16),
        scratch_shapes=[
            pltpu.VMEM((BLOCK + SPAN, HEAD_DIM_A), BF16), pltpu.VMEM((BLOCK + SPAN, HEAD_DIM_A), BF16),
            pltpu.VMEM((d4, BLOCK + n4, HEAD_DIM_A), BF16), pltpu.VMEM((d4, BLOCK + n4, HEAD_DIM_A), BF16),
            pltpu.VMEM((d16, 2 * BLOCK, HEAD_DIM_A), BF16), pltpu.VMEM((d16, 2 * BLOCK, HEAD_DIM_A), BF16),
            pltpu.VMEM((6, BLOCK, 2 * BLOCK), F32),
            pltpu.VMEM((3, SPAN, HEAD_DIM_A), F32),
            pltpu.VMEM((3, SPAN, HEAD_DIM_A), F32),
            pltpu.VMEM((2, SPAN, HEAD_DIM_A), F32),
            pltpu.VMEM((2, SPAN, HEAD_DIM_A), F32),
        ],
        compiler_params=_params(("arbitrary", "arbitrary"), 48),
        name="attn_dilated",
    )(*args)


def _attn_b_kernel(slope_ref, sink_ref, q_ref, k_ref, v_ref, kp_ref, vp_ref, o_ref, kb, vb, bias_ref):
    hk = pl.program_id(0)
    sp = pl.program_id(1)
    scale = HEAD_DIM_B ** -0.5

    for g in range(GQA_GROUP):
        b0, b1 = _band_bias(slope_ref[hk * GQA_GROUP + g], 1.0, MAX_DIST_B)
        bias_ref[2 * g] = b0
        bias_ref[2 * g + 1] = b1

    kb[0:BLOCK] = kp_ref[...]
    kb[BLOCK:] = k_ref[...]
    vb[0:BLOCK] = vp_ref[...]
    vb[BLOCK:] = v_ref[...]

    lane = lax.broadcasted_iota(jnp.int32, (BLOCK, 2 * HEAD_DIM_B), 1)
    low = lane < HEAD_DIM_B

    def body(b, carry):
        r0 = pl.multiple_of(b * BLOCK, BLOCK)
        first = ((sp == 0) & (b == 0)).astype(jnp.int32)
        kk = kb[pl.ds(r0, 2 * BLOCK), :]
        vv = vb[pl.ds(r0, 2 * BLOCK), :]
        for pr in range(GQA_GROUP // 2):
            q2 = q_ref[pl.ds(r0, BLOCK), pr * 128:(pr + 1) * 128]
            outs = []
            for half in range(2):
                g = 2 * pr + half
                keep = low if half == 0 else jnp.logical_not(low)
                qq = jnp.where(keep, q2, jnp.zeros_like(q2))
                s = _scores(qq, kk, scale, bias_ref[2 * g + first])
                sink = sink_ref[hk * GQA_GROUP + g]
                m = jnp.maximum(jnp.max(s, axis=-1, keepdims=True), sink)
                p = jnp.exp(s - m)
                l = jnp.sum(p, axis=-1, keepdims=True) + jnp.exp(sink - m)
                outs.append(jnp.dot(p.astype(BF16), vv, preferred_element_type=F32) * (1.0 / l))
            o_ref[pl.ds(r0, BLOCK), pr * 128:(pr + 1) * 128] = jnp.where(low, outs[0], outs[1]).astype(o_ref.dtype)
        return carry

    lax.fori_loop(0, SPAN // BLOCK, body, 0)


def _attn_b(qb, kv2, slopes_b, sinks):
    s = qb.shape[0]
    gw = GQA_GROUP * HEAD_DIM_B
    blk = SPAN // BLOCK
    nk = N_KV_HEADS_B

    def cur(off):
        return pl.BlockSpec((SPAN, 128), lambda h, sp: (sp, off + h))

    def prev(off):
        return pl.BlockSpec((BLOCK, 128), lambda h, sp: (jnp.maximum(sp * blk - 1, 0), off + h))

    smem = pl.BlockSpec(memory_space=pltpu.SMEM)
    return pl.pallas_call(
        _attn_b_kernel,
        grid=(nk, s // SPAN),
        in_specs=[smem, smem, pl.BlockSpec((SPAN, gw), lambda h, sp: (sp, h)),
                  cur(0), cur(nk), prev(0), prev(nk)],
        out_specs=pl.BlockSpec((SPAN, gw), lambda h, sp: (sp, h)),
        out_shape=jax.ShapeDtypeStruct((s, WIDTH_B), BF16),
        scratch_shapes=[pltpu.VMEM((BLOCK + SPAN, 128), BF16), pltpu.VMEM((BLOCK + SPAN, 128), BF16),
                        pltpu.VMEM((2 * GQA_GROUP, BLOCK, 2 * BLOCK), F32)],
        compiler_params=_params(("arbitrary", "arbitrary"), 48),
        name="attn_swa",
    )(slopes_b, sinks, qb, kv2, kv2, kv2, kv2)


def _mlp_kernel(u_ref, wg_ref, wv_ref, cwg_ref, cwv_ref, cbg_ref, cbv_ref, wd_ref, o_ref,
                hb0, hb1, carry, *, nj):
    t = pl.program_id(0)
    tm = u_ref.shape[0]
    tn = wg_ref.shape[1]
    td = jnp.maximum(t - 1, 0)
    i_d = td // nj
    j_d = td % nj

    @pl.when(t == 0)
    def _():
        hb1[...] = jnp.zeros(hb1.shape, F32)

    def step(hw, hr):
        hr[0:8, :] = jnp.where(i_d == 0, 0.0, carry[j_d])
        carry[j_d] = hr[tm:tm + 8, :]

        def conv(c0, c1, cw_ref, cb_ref):
            acc = cw_ref[0:1, :] * hr[8 - 2:8 - 2 + tm, c0:c1]
            acc = acc + cw_ref[1:2, :] * hr[8 - 1:8 - 1 + tm, c0:c1]
            acc = acc + cw_ref[2:3, :] * hr[8:8 + tm, c0:c1]
            return acc + cb_ref[...]

        gate = conv(0, tn, cwg_ref, cbg_ref)
        val = conv(tn, 2 * tn, cwv_ref, cbv_ref)
        act = (gate * jax.nn.sigmoid(gate) * val).astype(BF16)
        contrib = jnp.dot(act, wd_ref[...], preferred_element_type=F32)
        o_ref[...] = jnp.where(j_d == 0, contrib, o_ref[...] + contrib)

        u = u_ref[...]
        hw[8:, 0:tn] = jnp.dot(u, wg_ref[...], preferred_element_type=F32)
        hw[8:, tn:] = jnp.dot(u, wv_ref[...], preferred_element_type=F32)

    @pl.when(t % 2 == 0)
    def _():
        step(hb0, hb1)

    @pl.when(t % 2 == 1)
    def _():
        step(hb1, hb0)


def _conv_ffn(u, w_up, conv_w, conv_b, w_down):
    s, d = u.shape
    nj = FF_PAD // MLP_TN
    n_chunks = (s // MLP_TM) * nj
    last = n_chunks - 1

    def up_i(t):
        return jnp.minimum(t, last) // nj

    def up_j(t):
        return jnp.minimum(t, last) % nj

    def dn_i(t):
        return jnp.maximum(t - 1, 0) // nj

    def dn_j(t):
        return jnp.maximum(t - 1, 0) % nj

    return pl.pallas_call(
        functools.partial(_mlp_kernel, nj=nj),
        grid=(n_chunks + 1,),
        in_specs=[pl.BlockSpec((MLP_TM, d), lambda t: (up_i(t), 0)),
                  pl.BlockSpec((d, MLP_TN), lambda t: (0, up_j(t))),
                  pl.BlockSpec((d, MLP_TN), lambda t: (0, nj + up_j(t))),
                  pl.BlockSpec((CONV_WIDTH, MLP_TN), lambda t: (0, dn_j(t))),
                  pl.BlockSpec((CONV_WIDTH, MLP_TN), lambda t: (0, nj + dn_j(t))),
                  pl.BlockSpec((1, MLP_TN), lambda t: (0, dn_j(t))),
                  pl.BlockSpec((1, MLP_TN), lambda t: (0, nj + dn_j(t))),
                  pl.BlockSpec((MLP_TN, d), lambda t: (dn_j(t), 0))],
        out_specs=pl.BlockSpec((MLP_TM, d), lambda t: (dn_i(t), 0)),
        out_shape=jax.ShapeDtypeStruct((s, d), F32),
        scratch_shapes=[pltpu.VMEM((8 + MLP_TM, 2 * MLP_TN), F32),
                        pltpu.VMEM((8 + MLP_TM, 2 * MLP_TN), F32),
                        pltpu.VMEM((nj, 8, 2 * MLP_TN), F32)],
        compiler_params=_params(("arbitrary",), 60),
        name="conv_ffn",
    )(u, w_up, w_up, conv_w, conv_w, conv_b, conv_b, w_down)


def _pad_ff(a, axis):
    pad = [(0, 0)] * a.ndim
    pad[axis] = (0, FF_PAD - D_FF)
    if a.shape[axis] == D_FF:
        return jnp.pad(a, pad)
    g, v = jnp.split(a, 2, axis=axis)
    return jnp.concatenate([jnp.pad(g, pad), jnp.pad(v, pad)], axis=axis)


def _alibi_slopes():
    i = jnp.arange(1, N_ALIBI_HEADS + 1, dtype=F32)
    return jnp.exp2(-8.0 * i / N_ALIBI_HEADS)


def _dup_heads(w):
    k = w.shape[0]
    w = w.reshape(k, N_KV_HEADS_B, 1, HEAD_DIM_B)
    return jnp.broadcast_to(w, (k, N_KV_HEADS_B, 2, HEAD_DIM_B)).reshape(k, 2 * KV_WIDTH_B)


def kernel(x, c, w_mod, b_mod, w_in, sinks, w_out, ln1_g, ln1_b, w_up, conv_w, conv_b, w_down, ln2_g, ln2_b):
    b, s, d = x.shape
    assert (b, s, d) == (1, SEQ, D_MODEL)
    slopes = _alibi_slopes()
    slopes_b, slopes_a = slopes[:N_Q_HEADS_B], slopes[N_Q_HEADS_B:]
    mod = _mod_all(c, w_mod, b_mod)
    x = x.reshape(s, d)
    u = None
    kv_off = QKV_A + WIDTH_B
    for l in range(DEPTH):
        shift_a, scale_a, gate_a, shift_m, scale_m, gate_m = (
            mod[l, :, n * d:(n + 1) * d] for n in range(N_MOD))
        w_kv2 = jnp.concatenate([_dup_heads(w_in[l, :, kv_off:kv_off + KV_WIDTH_B]),
                                 _dup_heads(w_in[l, :, kv_off + KV_WIDTH_B:])], axis=1)
        w_up_l = _pad_ff(w_up[l], 1).astype(BF16)
        w_down_l = _pad_ff(w_down[l], 0).astype(BF16)
        conv_w_l = _pad_ff(conv_w[l], 1)
        conv_b_l = _pad_ff(conv_b[l].reshape(1, 2 * D_FF), 1)

        if u is None:
            u = _modulate(x, scale_a, shift_a)
        h1, h4, h16 = _in_proj_dilated(u, w_in, l, QKV_A)
        qb = _matmul([u], w_in, l, QKV_A // MM_TN, WIDTH_B, BF16, "in_proj_qb")
        kv2 = _matmul([u], w_kv2, l, 0, 4 * KV_WIDTH_B, BF16, "in_proj_kvb")
        mixed_a = _attn_a(h1, h4, h16, slopes_a)
        mixed_b = _attn_b(qb, kv2, slopes_b, sinks[l])
        att = _matmul([mixed_a, mixed_b], w_out, l, 0, d, F32, "out_proj")
        x, u = _residual_ln(x, att, gate_a, ln1_g[l].reshape(1, d), ln1_b[l].reshape(1, d),
                            (scale_m, shift_m))
        y = _conv_ffn(u, w_up_l, conv_w_l, conv_b_l, w_down_l)
        nxt = None
        if l + 1 < DEPTH:
            nxt = (mod[l + 1, :, d:2 * d], mod[l + 1, :, 0:d])
        x, u = _residual_ln(x, y, gate_m, ln2_g[l].reshape(1, d), ln2_b[l].reshape(1, d), nxt)
    return x.reshape(b, s, d)
```

```python
import functools
import math

import jax
import jax.numpy as jnp
from jax import lax
from jax.experimental import pallas as pl
from jax.experimental.pallas import tpu as pltpu

F32 = jnp.float32
BF16 = jnp.bfloat16

D_MODEL = 4096
SEQ = 8192
DEPTH = 2
HEAD_DIM_A = 128
N_HEADS_A = 16
DILATIONS = (1, 4, 16)
MAX_DIST_A = 128
HEAD_DIM_B = 64
N_Q_HEADS_B = 32
N_KV_HEADS_B = 4
GQA_GROUP = 8
MAX_DIST_B = 127
WIDTH_A = N_HEADS_A * HEAD_DIM_A
WIDTH_B = N_Q_HEADS_B * HEAD_DIM_B
KV_WIDTH_B = N_KV_HEADS_B * HEAD_DIM_B
QKV_A = 3 * WIDTH_A
N_ALIBI_HEADS = N_HEADS_A + N_Q_HEADS_B
D_FF = 11008
CONV_WIDTH = 3
N_MOD = 6
DEEPNORM_ALPHA = (2 * DEPTH) ** 0.25
LN_EPS = 1e-5
BLOCK = 128
NEG = -1e30

SPAN = 2048
MM_TM = 1024
MM_TN = 512
MLP_TM = 512
MLP_TN = 512
FF_PAD = MLP_TN * math.ceil(D_FF / MLP_TN)
LN_TM = 256
CAST_BLK = 256
ATTN_UNROLL = 16
MOD_TN = 512
MIB = 1024 * 1024


def _params(sem, vmem_mib):
    return pltpu.CompilerParams(dimension_semantics=sem, vmem_limit_bytes=vmem_mib * MIB)


def _mod_kernel(c_ref, w_ref, b_ref, o_ref, cb_ref):
    @pl.when((pl.program_id(0) == 0) & (pl.program_id(1) == 0))
    def _():
        cc = c_ref[...]
        cb_ref[...] = jnp.broadcast_to(cc * jax.nn.sigmoid(cc), cb_ref.shape)

    k_dim = w_ref.shape[1]
    n_grp = w_ref.shape[2] // 128

    def body(k, accs):
        r = pl.multiple_of(k * 8, 8)
        cb = cb_ref[pl.ds(r, 8), :]
        return tuple(a + w_ref[0, pl.ds(r, 8), g * 128:(g + 1) * 128] * cb
                     for g, a in enumerate(accs))

    accs = lax.fori_loop(0, k_dim // 8, body,
                         tuple(jnp.zeros((8, 128), F32) for _ in range(n_grp)), unroll=8)
    for g, a in enumerate(accs):
        o_ref[0, :, g * 128:(g + 1) * 128] = (jnp.sum(a, axis=0, keepdims=True)
                                              + b_ref[0, :, g * 128:(g + 1) * 128])


def _mod_all(c, w_mod, b_mod):
    depth, d, n = w_mod.shape
    return pl.pallas_call(
        _mod_kernel,
        grid=(depth, n // MOD_TN),
        in_specs=[pl.BlockSpec((d, 1), lambda l, j: (0, 0)),
                  pl.BlockSpec((1, d, MOD_TN), lambda l, j: (l, 0, j)),
                  pl.BlockSpec((1, 1, MOD_TN), lambda l, j: (l, 0, j))],
        out_specs=pl.BlockSpec((1, 1, MOD_TN), lambda l, j: (l, 0, j)),
        out_shape=jax.ShapeDtypeStruct((depth, 1, n), F32),
        scratch_shapes=[pltpu.VMEM((d, 128), F32)],
        compiler_params=_params(("arbitrary", "arbitrary"), 40),
        name="mod_matvec",
    )(c.reshape(d, 1), w_mod, b_mod.reshape(depth, 1, n))


def _modulate_kernel(x_ref, sc_ref, sh_ref, u_ref):
    u_ref[...] = (x_ref[...] * (1.0 + sc_ref[...]) + sh_ref[...]).astype(u_ref.dtype)


def _modulate(x, scale, shift):
    s, d = x.shape
    row = pl.BlockSpec((1, d), lambda i: (0, 0))
    return pl.pallas_call(
        _modulate_kernel,
        grid=(s // LN_TM,),
        in_specs=[pl.BlockSpec((LN_TM, d), lambda i: (i, 0)), row, row],
        out_specs=pl.BlockSpec((LN_TM, d), lambda i: (i, 0)),
        out_shape=jax.ShapeDtypeStruct((s, d), BF16),
        compiler_params=_params(("arbitrary",), 32),
        name="modulate",
    )(x, scale, shift)


def _ln_kernel(x_ref, y_ref, gate_ref, g_ref, b_ref, *rest, with_next):
    z = DEEPNORM_ALPHA * x_ref[...] + (1.0 + gate_ref[...]) * y_ref[...]
    mu = jnp.mean(z, axis=-1, keepdims=True)
    zc = z - mu
    var = jnp.mean(zc * zc, axis=-1, keepdims=True)
    xn = zc * lax.rsqrt(var + LN_EPS) * g_ref[...] + b_ref[...]
    if with_next:
        sc_ref, sh_ref, xo_ref, uo_ref = rest
        xo_ref[...] = xn
        uo_ref[...] = (xn * (1.0 + sc_ref[...]) + sh_ref[...]).astype(uo_ref.dtype)
    else:
        (xo_ref,) = rest
        xo_ref[...] = xn


def _residual_ln(x, y, gate, g, b, nxt=None):
    s, d = x.shape
    row = pl.BlockSpec((1, d), lambda i: (0, 0))
    tile = pl.BlockSpec((LN_TM, d), lambda i: (i, 0))
    ins = [x, y, gate, g, b]
    in_specs = [tile, tile, row, row, row]
    out_shape = [jax.ShapeDtypeStruct((s, d), F32)]
    out_specs = [tile]
    if nxt is not None:
        ins += list(nxt)
        in_specs += [row, row]
        out_shape.append(jax.ShapeDtypeStruct((s, d), BF16))
        out_specs.append(tile)
    res = pl.pallas_call(
        functools.partial(_ln_kernel, with_next=nxt is not None),
        grid=(s // LN_TM,),
        in_specs=in_specs, out_specs=out_specs, out_shape=out_shape,
        compiler_params=_params(("arbitrary",), 48),
        name="residual_ln",
    )(*ins)
    return res if nxt is not None else (res[0], None)


def _mm_kernel(*refs, n_lhs):
    lhs = refs[:n_lhs]
    w_ref, o_ref, wb_ref = refs[n_lhs:]

    @pl.when(pl.program_id(1) == 0)
    def _():
        wb_ref[...] = w_ref[...].astype(wb_ref.dtype)

    acc = None
    k0 = 0
    for a in lhs:
        kk = a.shape[1]
        part = jnp.dot(a[...], wb_ref[k0:k0 + kk, :], preferred_element_type=F32)
        acc = part if acc is None else acc + part
        k0 += kk
    o_ref[...] = acc.astype(o_ref.dtype)


def _mm_dilated_kernel(u_ref, w_ref, o1_ref, o4_ref, o16_ref, wb_ref, acc_ref):
    @pl.when(pl.program_id(1) == 0)
    def _():
        wb_ref[...] = w_ref[...].astype(wb_ref.dtype)

    acc = jnp.dot(u_ref[...], wb_ref[...], preferred_element_type=F32)
    o1_ref[...] = acc.astype(o1_ref.dtype)
    n_cb, tm, _ = acc_ref.shape
    for cb in range(n_cb):
        acc_ref[cb] = acc[:, cb * 128:(cb + 1) * 128]
    for o_ref, d in ((o4_ref, DILATIONS[1]), (o16_ref, DILATIONS[2])):
        for r in range(d):
            for cb in range(n_cb):
                o_ref[r, :, cb * 128:(cb + 1) * 128] = (
                    acc_ref[cb, pl.ds(r, tm // d, stride=d), :].astype(o_ref.dtype))


def _weight_spec(w, layer, col_block0):
    k = w.shape[-2]
    if w.ndim == 3:
        return pl.BlockSpec((None, k, MM_TN), lambda j, i: (layer, 0, col_block0 + j))
    return pl.BlockSpec((k, MM_TN), lambda j, i: (0, col_block0 + j))


def _in_proj_dilated(u, w, layer, n_cols):
    s, k = u.shape
    d4, d16 = DILATIONS[1], DILATIONS[2]
    return pl.pallas_call(
        _mm_dilated_kernel,
        grid=(n_cols // MM_TN, s // MM_TM),
        in_specs=[pl.BlockSpec((MM_TM, k), lambda j, i: (i, 0)), _weight_spec(w, layer, 0)],
        out_specs=[pl.BlockSpec((MM_TM, MM_TN), lambda j, i: (i, j)),
                   pl.BlockSpec((d4, MM_TM // d4, MM_TN), lambda j, i: (0, i, j)),
                   pl.BlockSpec((d16, MM_TM // d16, MM_TN), lambda j, i: (0, i, j))],
        out_shape=[jax.ShapeDtypeStruct((s, n_cols), BF16),
                   jax.ShapeDtypeStruct((d4, s // d4, n_cols), BF16),
                   jax.ShapeDtypeStruct((d16, s // d16, n_cols), BF16)],
        scratch_shapes=[pltpu.VMEM((k, MM_TN), BF16),
                        pltpu.VMEM((MM_TN // 128, MM_TM, 128), F32)],
        compiler_params=_params(("arbitrary", "arbitrary"), 56),
        name="in_proj_dilated",
    )(u, w)


def _matmul(lhs, w, layer, col_block0, n_cols, out_dtype, name):
    s = lhs[0].shape[0]
    k = w.shape[-2]
    assert sum(a.shape[1] for a in lhs) == k
    return pl.pallas_call(
        functools.partial(_mm_kernel, n_lhs=len(lhs)),
        grid=(n_cols // MM_TN, s // MM_TM),
        in_specs=[pl.BlockSpec((MM_TM, a.shape[1]), lambda j, i: (i, 0)) for a in lhs]
        + [_weight_spec(w, layer, col_block0)],
        out_specs=pl.BlockSpec((MM_TM, MM_TN), lambda j, i: (i, j)),
        out_shape=jax.ShapeDtypeStruct((s, n_cols), out_dtype),
        scratch_shapes=[pltpu.VMEM((k, MM_TN), BF16)],
        compiler_params=_params(("arbitrary", "arbitrary"), 56),
        name=name,
    )(*lhs, w)


def _band_bias(slope, dist_scale, max_dist):
    qi = lax.broadcasted_iota(jnp.int32, (BLOCK, 2 * BLOCK), 0)
    kj = lax.broadcasted_iota(jnp.int32, (BLOCK, 2 * BLOCK), 1)
    dist = qi + BLOCK - kj
    band = (dist >= 0) & (dist <= max_dist)
    bias = jnp.where(band, -(slope * dist_scale) * dist.astype(F32), NEG)
    return bias, jnp.where(kj >= BLOCK, bias, NEG)


def _scores(q, kk, scale, bias):
    s = lax.dot_general(q, kk, (((1,), (1,)), ((), ())), preferred_element_type=F32)
    return s * scale + bias


def _attn_a_kernel(slope_ref,
                   q1_ref, k1_ref, v1_ref, kp1_ref, vp1_ref,
                   q4_ref, k4_ref, v4_ref, kp4_ref, vp4_ref,
                   q16_ref, k16_ref, v16_ref, kp16_ref, vp16_ref,
                   o_ref,
                   kb1, vb1, kb4, vb4, kb16, vb16, bias_ref, acc_ref, lse_ref, nat_ref, lsen_ref):
    head = pl.program_id(0)
    sp = pl.program_id(1)
    slope = slope_ref[head]
    scale = HEAD_DIM_A ** -0.5
    d4, d16 = DILATIONS[1], DILATIONS[2]
    n4 = SPAN // d4

    for bi, d in enumerate(DILATIONS):
        b0, b1 = _band_bias(slope, float(d), MAX_DIST_A)
        bias_ref[2 * bi] = b0
        bias_ref[2 * bi + 1] = b1

    kb1[0:BLOCK] = kp1_ref[...]
    kb1[BLOCK:] = k1_ref[...]
    vb1[0:BLOCK] = vp1_ref[...]
    vb1[BLOCK:] = v1_ref[...]
    kb4[:, 0:BLOCK] = kp4_ref[...]
    kb4[:, BLOCK:] = k4_ref[...]
    vb4[:, 0:BLOCK] = vp4_ref[...]
    vb4[:, BLOCK:] = v4_ref[...]
    kb16[:, 0:BLOCK] = kp16_ref[...]
    kb16[:, BLOCK:] = k16_ref[...]
    vb16[:, 0:BLOCK] = vp16_ref[...]
    vb16[:, BLOCK:] = v16_ref[...]

    def block(q, kk, vv, bias):
        s = _scores(q, kk, scale, bias)
        m = jnp.max(s, axis=-1, keepdims=True)
        p = jnp.exp(s - m)
        l = jnp.sum(p, axis=-1, keepdims=True)
        o = jnp.dot(p.astype(BF16), vv, preferred_element_type=F32) * (1.0 / l)
        return o, jnp.broadcast_to(m + jnp.log(l), o.shape)

    def body1(b, carry):
        r0 = pl.multiple_of(b * BLOCK, BLOCK)
        first = ((sp == 0) & (b == 0)).astype(jnp.int32)
        o, lse = block(q1_ref[pl.ds(r0, BLOCK), :], kb1[pl.ds(r0, 2 * BLOCK), :],
                       vb1[pl.ds(r0, 2 * BLOCK), :], bias_ref[first])
        acc_ref[0, pl.ds(r0, BLOCK), :] = o
        lse_ref[0, pl.ds(r0, BLOCK), :] = lse
        return carry

    lax.fori_loop(0, SPAN // BLOCK, body1, 0, unroll=ATTN_UNROLL)

    def body4(idx, carry):
        r = idx // (n4 // BLOCK)
        b = idx % (n4 // BLOCK)
        r0 = pl.multiple_of(b * BLOCK, BLOCK)
        first = ((sp == 0) & (b == 0)).astype(jnp.int32)
        o, lse = block(q4_ref[r, pl.ds(r0, BLOCK), :], kb4[r, pl.ds(r0, 2 * BLOCK), :],
                       vb4[r, pl.ds(r0, 2 * BLOCK), :], bias_ref[2 + first])
        w0 = pl.multiple_of(idx * BLOCK, BLOCK)
        acc_ref[1, pl.ds(w0, BLOCK), :] = o
        lse_ref[1, pl.ds(w0, BLOCK), :] = lse
        return carry

    lax.fori_loop(0, SPAN // BLOCK, body4, 0, unroll=ATTN_UNROLL)

    def body16(r, carry):
        first = (sp == 0).astype(jnp.int32)
        o, lse = block(q16_ref[r], kb16[r], vb16[r], bias_ref[4 + first])
        w0 = pl.multiple_of(r * BLOCK, BLOCK)
        acc_ref[2, pl.ds(w0, BLOCK), :] = o
        lse_ref[2, pl.ds(w0, BLOCK), :] = lse
        return carry

    lax.fori_loop(0, d16, body16, 0, unroll=ATTN_UNROLL)

    for r in range(d4):
        nat_ref[0, pl.ds(r, n4, stride=d4), :] = acc_ref[1, r * n4:(r + 1) * n4, :]
        lsen_ref[0, pl.ds(r, n4, stride=d4), :] = lse_ref[1, r * n4:(r + 1) * n4, :]
    for r in range(d16):
        nat_ref[1, pl.ds(r, BLOCK, stride=d16), :] = acc_ref[2, r * BLOCK:(r + 1) * BLOCK, :]
        lsen_ref[1, pl.ds(r, BLOCK, stride=d16), :] = lse_ref[2, r * BLOCK:(r + 1) * BLOCK, :]

    chunk = 256

    def merge(c, carry):
        r0 = pl.multiple_of(c * chunk, chunk)
        rows = pl.ds(r0, chunk)
        l1 = lse_ref[0, rows, :]
        l4 = lsen_ref[0, rows, :]
        l16 = lsen_ref[1, rows, :]
        mx = jnp.maximum(jnp.maximum(l1, l4), l16)
        e1 = jnp.exp(l1 - mx)
        e4 = jnp.exp(l4 - mx)
        e16 = jnp.exp(l16 - mx)
        inv = 1.0 / (e1 + e4 + e16)
        o = ((e1 * inv) * acc_ref[0, rows, :] + (e4 * inv) * nat_ref[0, rows, :]
             + (e16 * inv) * nat_ref[1, rows, :])
        o_ref[rows, :] = o.astype(o_ref.dtype)
        return carry

    lax.fori_loop(0, SPAN // chunk, merge, 0)


def _attn_a(h1, h4, h16, slopes_a):
    s = h1.shape[0]
    d4, d16 = DILATIONS[1], DILATIONS[2]
    n4 = SPAN // d4
    nh = N_HEADS_A
    blk1 = SPAN // BLOCK

    def cur1(off):
        return pl.BlockSpec((SPAN, HEAD_DIM_A), lambda h, sp: (sp, off + h))

    def prev1(off):
        return pl.BlockSpec((BLOCK, HEAD_DIM_A), lambda h, sp: (jnp.maximum(sp * blk1 - 1, 0), off + h))

    def cur4(off):
        return pl.BlockSpec((d4, n4, HEAD_DIM_A), lambda h, sp: (0, sp, off + h))

    def prev4(off):
        return pl.BlockSpec((d4, BLOCK, HEAD_DIM_A),
                            lambda h, sp: (0, jnp.maximum(sp * (n4 // BLOCK) - 1, 0), off + h))

    def cur16(off):
        return pl.BlockSpec((d16, BLOCK, HEAD_DIM_A), lambda h, sp: (0, sp, off + h))

    def prev16(off):
        return pl.BlockSpec((d16, BLOCK, HEAD_DIM_A), lambda h, sp: (0, jnp.maximum(sp - 1, 0), off + h))

    in_specs = [pl.BlockSpec(memory_space=pltpu.SMEM)]
    args = [slopes_a]
    for arr, cur, prev in ((h1, cur1, prev1), (h4, cur4, prev4), (h16, cur16, prev16)):
        in_specs += [cur(0), cur(nh), cur(2 * nh), prev(nh), prev(2 * nh)]
        args += [arr] * 5
    return pl.pallas_call(
        _attn_a_kernel,
        grid=(nh, s // SPAN),
        in_specs=in_specs,
        out_specs=pl.BlockSpec((SPAN, HEAD_DIM_A), lambda h, sp: (sp, h)),
        out_shape=jax.ShapeDtypeStruct((s, WIDTH_A), BF16),
        scratch_shapes=[
            pltpu.VMEM((BLOCK + SPAN, HEAD_DIM_A), BF16), pltpu.VMEM((BLOCK + SPAN, HEAD_DIM_A), BF16),
            pltpu.VMEM((d4, BLOCK + n4, HEAD_DIM_A), BF16), pltpu.VMEM((d4, BLOCK + n4, HEAD_DIM_A), BF16),
            pltpu.VMEM((d16, 2 * BLOCK, HEAD_DIM_A), BF16), pltpu.VMEM((d16, 2 * BLOCK, HEAD_DIM_A), BF16),
            pltpu.VMEM((6, BLOCK, 2 * BLOCK), F32),
            pltpu.VMEM((3, SPAN, HEAD_DIM_A), F32),
            pltpu.VMEM((3, SPAN, HEAD_DIM_A), F32),
            pltpu.VMEM((2, SPAN, HEAD_DIM_A), F32),
            pltpu.VMEM((2, SPAN, HEAD_DIM_A), F32),
        ],
        compiler_params=_params(("arbitrary", "arbitrary"), 48),
        name="attn_dilated",
    )(*args)


def _attn_b_kernel(slope_ref, sink_ref, q_ref, k_ref, v_ref, kp_ref, vp_ref, o_ref, kb, vb, bias_ref):
    hk = pl.program_id(0)
    sp = pl.program_id(1)
    scale = HEAD_DIM_B ** -0.5

    for g in range(GQA_GROUP):
        b0, b1 = _band_bias(slope_ref[hk * GQA_GROUP + g], 1.0, MAX_DIST_B)
        bias_ref[2 * g] = b0
        bias_ref[2 * g + 1] = b1

    src = lax.broadcasted_iota(jnp.int32, (128, 128), 0)
    dst = lax.broadcasted_iota(jnp.int32, (128, 128), 1)
    pick = (src == (hk % 2) * HEAD_DIM_B + dst % HEAD_DIM_B).astype(BF16)

    def dup(ref):
        return jnp.dot(ref[...], pick, preferred_element_type=F32).astype(BF16)

    kb[0:BLOCK] = dup(kp_ref)
    kb[BLOCK:] = dup(k_ref)
    vb[0:BLOCK] = dup(vp_ref)
    vb[BLOCK:] = dup(v_ref)

    lane = lax.broadcasted_iota(jnp.int32, (BLOCK, 2 * HEAD_DIM_B), 1)
    low = lane < HEAD_DIM_B

    def body(b, carry):
        r0 = pl.multiple_of(b * BLOCK, BLOCK)
        first = ((sp == 0) & (b == 0)).astype(jnp.int32)
        kk = kb[pl.ds(r0, 2 * BLOCK), :]
        vv = vb[pl.ds(r0, 2 * BLOCK), :]
        for pr in range(GQA_GROUP // 2):
            q2 = q_ref[pl.ds(r0, BLOCK), pr * 128:(pr + 1) * 128]
            outs = []
            for half in range(2):
                g = 2 * pr + half
                keep = low if half == 0 else jnp.logical_not(low)
                qq = jnp.where(keep, q2, jnp.zeros_like(q2))
                s = _scores(qq, kk, scale, bias_ref[2 * g + first])
                sink = sink_ref[hk * GQA_GROUP + g]
                m = jnp.maximum(jnp.max(s, axis=-1, keepdims=True), sink)
                p = jnp.exp(s - m)
                l = jnp.sum(p, axis=-1, keepdims=True) + jnp.exp(sink - m)
                outs.append(jnp.dot(p.astype(BF16), vv, preferred_element_type=F32) * (1.0 / l))
            o_ref[pl.ds(r0, BLOCK), pr * 128:(pr + 1) * 128] = jnp.where(low, outs[0], outs[1]).astype(o_ref.dtype)
        return carry

    lax.fori_loop(0, SPAN // BLOCK, body, 0, unroll=4)


def _attn_b(hb, slopes_b, sinks):
    s = hb.shape[0]
    gw = GQA_GROUP * HEAD_DIM_B
    blk = SPAN // BLOCK
    k_off = WIDTH_B // 128
    v_off = k_off + KV_WIDTH_B // 128

    def cur(off):
        return pl.BlockSpec((SPAN, 128), lambda h, sp: (sp, off + h // 2))

    def prev(off):
        return pl.BlockSpec((BLOCK, 128), lambda h, sp: (jnp.maximum(sp * blk - 1, 0), off + h // 2))

    smem = pl.BlockSpec(memory_space=pltpu.SMEM)
    return pl.pallas_call(
        _attn_b_kernel,
        grid=(N_KV_HEADS_B, s // SPAN),
        in_specs=[smem, smem, pl.BlockSpec((SPAN, gw), lambda h, sp: (sp, h)),
                  cur(k_off), cur(v_off), prev(k_off), prev(v_off)],
        out_specs=pl.BlockSpec((SPAN, gw), lambda h, sp: (sp, h)),
        out_shape=jax.ShapeDtypeStruct((s, WIDTH_B), BF16),
        scratch_shapes=[pltpu.VMEM((BLOCK + SPAN, 128), BF16), pltpu.VMEM((BLOCK + SPAN, 128), BF16),
                        pltpu.VMEM((2 * GQA_GROUP, BLOCK, 2 * BLOCK), F32)],
        compiler_params=_params(("arbitrary", "arbitrary"), 48),
        name="attn_swa",
    )(slopes_b, sinks, hb, hb, hb, hb, hb)


def _mlp_kernel(u_ref, wg_ref, wv_ref, cwg_ref, cwv_ref, cbg_ref, cbv_ref, wd_ref, o_ref,
                hb0, hb1, carry, *, nj):
    t = pl.program_id(0)
    tm = u_ref.shape[0]
    tn = wg_ref.shape[1]
    td = jnp.maximum(t - 1, 0)
    i_d = td // nj
    j_d = td % nj

    @pl.when(t == 0)
    def _():
        hb1[...] = jnp.zeros(hb1.shape, F32)

    def step(hw, hr):
        hr[0:8, :] = jnp.where(i_d == 0, 0.0, carry[j_d])
        carry[j_d] = hr[tm:tm + 8, :]

        def conv(c0, c1, cw_ref, cb_ref):
            acc = cw_ref[0:1, :] * hr[8 - 2:8 - 2 + tm, c0:c1]
            acc = acc + cw_ref[1:2, :] * hr[8 - 1:8 - 1 + tm, c0:c1]
            acc = acc + cw_ref[2:3, :] * hr[8:8 + tm, c0:c1]
            return acc + cb_ref[...]

        gate = conv(0, tn, cwg_ref, cbg_ref)
        val = conv(tn, 2 * tn, cwv_ref, cbv_ref)
        act = (gate * jax.nn.sigmoid(gate) * val).astype(BF16)
        contrib = jnp.dot(act, wd_ref[...], preferred_element_type=F32)
        o_ref[...] = jnp.where(j_d == 0, contrib, o_ref[...] + contrib)

        u = u_ref[...]
        hw[8:, 0:tn] = jnp.dot(u, wg_ref[...], preferred_element_type=F32)
        hw[8:, tn:] = jnp.dot(u, wv_ref[...], preferred_element_type=F32)

    @pl.when(t % 2 == 0)
    def _():
        step(hb0, hb1)

    @pl.when(t % 2 == 1)
    def _():
        step(hb1, hb0)


def _conv_ffn(u, w_up, conv_w, conv_b, w_down):
    s, d = u.shape
    nj = FF_PAD // MLP_TN
    n_chunks = (s // MLP_TM) * nj
    last = n_chunks - 1

    def up_i(t):
        return jnp.minimum(t, last) // nj

    def up_j(t):
        return jnp.minimum(t, last) % nj

    def dn_i(t):
        return jnp.maximum(t - 1, 0) // nj

    def dn_j(t):
        return jnp.maximum(t - 1, 0) % nj

    return pl.pallas_call(
        functools.partial(_mlp_kernel, nj=nj),
        grid=(n_chunks + 1,),
        in_specs=[pl.BlockSpec((MLP_TM, d), lambda t: (up_i(t), 0)),
                  pl.BlockSpec((d, MLP_TN), lambda t: (0, up_j(t))),
                  pl.BlockSpec((d, MLP_TN), lambda t: (0, nj + up_j(t))),
                  pl.BlockSpec((CONV_WIDTH, MLP_TN), lambda t: (0, dn_j(t))),
                  pl.BlockSpec((CONV_WIDTH, MLP_TN), lambda t: (0, nj + dn_j(t))),
                  pl.BlockSpec((1, MLP_TN), lambda t: (0, dn_j(t))),
                  pl.BlockSpec((1, MLP_TN), lambda t: (0, nj + dn_j(t))),
                  pl.BlockSpec((MLP_TN, d), lambda t: (dn_j(t), 0))],
        out_specs=pl.BlockSpec((MLP_TM, d), lambda t: (dn_i(t), 0)),
        out_shape=jax.ShapeDtypeStruct((s, d), F32),
        scratch_shapes=[pltpu.VMEM((8 + MLP_TM, 2 * MLP_TN), F32),
                        pltpu.VMEM((8 + MLP_TM, 2 * MLP_TN), F32),
                        pltpu.VMEM((nj, 8, 2 * MLP_TN), F32)],
        compiler_params=_params(("arbitrary",), 60),
        name="conv_ffn",
    )(u, w_up, w_up, conv_w, conv_w, conv_b, conv_b, w_down)


def _pad_ff(a, axis):
    pad = [(0, 0)] * a.ndim
    pad[axis] = (0, FF_PAD - D_FF)
    g, v = jnp.split(a, 2, axis=axis)
    return jnp.concatenate([jnp.pad(g, pad), jnp.pad(v, pad)], axis=axis)


def _cast_pad_kernel(w_ref, o_ref, *, axis, n_valid):
    c = pl.program_id(axis)

    @pl.when(c < n_valid)
    def _():
        o_ref[...] = w_ref[...].astype(o_ref.dtype)

    @pl.when(c >= n_valid)
    def _():
        o_ref[...] = jnp.zeros(o_ref.shape, o_ref.dtype)


def _cast_w_up(w_up, layer):
    d = w_up.shape[1]
    nv, npad = D_FF // CAST_BLK, FF_PAD // CAST_BLK
    return pl.pallas_call(
        functools.partial(_cast_pad_kernel, axis=1, n_valid=nv),
        grid=(2, npad),
        in_specs=[pl.BlockSpec((None, d, CAST_BLK),
                               lambda h, c: (layer, 0, h * nv + jnp.minimum(c, nv - 1)))],
        out_specs=pl.BlockSpec((d, CAST_BLK), lambda h, c: (0, h * npad + c)),
        out_shape=jax.ShapeDtypeStruct((d, 2 * FF_PAD), BF16),
        compiler_params=_params(("arbitrary", "arbitrary"), 32),
        name="cast_w_up",
    )(w_up)


def _cast_w_down(w_down, layer):
    d = w_down.shape[2]
    nv, npad = D_FF // CAST_BLK, FF_PAD // CAST_BLK
    return pl.pallas_call(
        functools.partial(_cast_pad_kernel, axis=0, n_valid=nv),
        grid=(npad,),
        in_specs=[pl.BlockSpec((None, CAST_BLK, d), lambda c: (layer, jnp.minimum(c, nv - 1), 0))],
        out_specs=pl.BlockSpec((CAST_BLK, d), lambda c: (c, 0)),
        out_shape=jax.ShapeDtypeStruct((FF_PAD, d), BF16),
        compiler_params=_params(("arbitrary",), 32),
        name="cast_w_down",
    )(w_down)


def _alibi_slopes():
    i = jnp.arange(1, N_ALIBI_HEADS + 1, dtype=F32)
    return jnp.exp2(-8.0 * i / N_ALIBI_HEADS)


def kernel(x, c, w_mod, b_mod, w_in, sinks, w_out, ln1_g, ln1_b, w_up, conv_w, conv_b, w_down, ln2_g, ln2_b):
    b, s, d = x.shape
    assert (b, s, d) == (1, SEQ, D_MODEL)
    slopes = _alibi_slopes()
    slopes_b, slopes_a = slopes[:N_Q_HEADS_B], slopes[N_Q_HEADS_B:]
    mod = _mod_all(c, w_mod, b_mod)
    x = x.reshape(s, d)
    u = None
    for l in range(DEPTH):
        shift_a, scale_a, gate_a, shift_m, scale_m, gate_m = (
            mod[l, :, n * d:(n + 1) * d] for n in range(N_MOD))
        w_up_l = _cast_w_up(w_up, l)
        w_down_l = _cast_w_down(w_down, l)
        conv_w_l = _pad_ff(conv_w[l], 1)
        conv_b_l = _pad_ff(conv_b[l].reshape(1, 2 * D_FF), 1)

        if u is None:
            u = _modulate(x, scale_a, shift_a)
        h1, h4, h16 = _in_proj_dilated(u, w_in, l, QKV_A)
        hb = _matmul([u], w_in, l, QKV_A // MM_TN, WIDTH_B + 2 * KV_WIDTH_B, BF16, "in_proj_b")
        mixed_a = _attn_a(h1, h4, h16, slopes_a)
        mixed_b = _attn_b(hb, slopes_b, sinks[l])
        att = _matmul([mixed_a, mixed_b], w_out, l, 0, d, F32, "out_proj")
        x, u = _residual_ln(x, att, gate_a, ln1_g[l].reshape(1, d), ln1_b[l].reshape(1, d),
                            (scale_m, shift_m))
        y = _conv_ffn(u, w_up_l, conv_w_l, conv_b_l, w_down_l)
        nxt = None
        if l + 1 < DEPTH:
            nxt = (mod[l + 1, :, d:2 * d], mod[l + 1, :, 0:d])
        x, u = _residual_ln(x, y, gate_m, ln2_g[l].reshape(1, d), ln2_b[l].reshape(1, d), nxt)
    return x.reshape(b, s, d)
```

```python
import functools
import math

import jax
import jax.numpy as jnp
from jax import lax
from jax.experimental import pallas as pl
from jax.experimental.pallas import tpu as pltpu

F32 = jnp.float32
BF16 = jnp.bfloat16

D_MODEL = 4096
SEQ = 8192
DEPTH = 2
HEAD_DIM_A = 128
N_HEADS_A = 16
DILATIONS = (1, 4, 16)
MAX_DIST_A = 128
HEAD_DIM_B = 64
N_Q_HEADS_B = 32
N_KV_HEADS_B = 4
GQA_GROUP = 8
MAX_DIST_B = 127
WIDTH_A = N_HEADS_A * HEAD_DIM_A
WIDTH_B = N_Q_HEADS_B * HEAD_DIM_B
KV_WIDTH_B = N_KV_HEADS_B * HEAD_DIM_B
QKV_A = 3 * WIDTH_A
N_ALIBI_HEADS = N_HEADS_A + N_Q_HEADS_B
D_FF = 11008
CONV_WIDTH = 3
N_MOD = 6
DEEPNORM_ALPHA = (2 * DEPTH) ** 0.25
LN_EPS = 1e-5
BLOCK = 128
NEG = -1e30

SPAN = 2048
MM_TM = 1024
MM_TN = 512
MLP_TM = 512
MLP_TN = 512
MLP_COLS = 256
FF_PAD =MLP_TN * math.ceil(D_FF / MLP_TN)
LN_TM = 256
CAST_BLK = 256
ATTN_HEADS = 2
ATTN_UNROLL = 16
MOD_TN = 512
MIB = 1024 * 1024


def _params(sem, vmem_mib):
    return pltpu.CompilerParams(dimension_semantics=sem, vmem_limit_bytes=vmem_mib * MIB)


def _mod_kernel(c_ref, w_ref, b_ref, o_ref, cb_ref):
    @pl.when((pl.program_id(0) == 0) & (pl.program_id(1) == 0))
    def _():
        cc = c_ref[...]
        cb_ref[...] = jnp.broadcast_to(cc * jax.nn.sigmoid(cc), cb_ref.shape)

    k_dim = w_ref.shape[1]
    n_grp = w_ref.shape[2] // 128

    def body(k, accs):
        r = pl.multiple_of(k * 8, 8)
        cb = cb_ref[pl.ds(r, 8), :]
        return tuple(a + w_ref[0, pl.ds(r, 8), g * 128:(g + 1) * 128] * cb
                     for g, a in enumerate(accs))

    accs = lax.fori_loop(0, k_dim // 8, body,
                         tuple(jnp.zeros((8, 128), F32) for _ in range(n_grp)), unroll=8)
    for g, a in enumerate(accs):
        o_ref[0, :, g * 128:(g + 1) * 128] = (jnp.sum(a, axis=0, keepdims=True)
                                              + b_ref[0, :, g * 128:(g + 1) * 128])


def _mod_all(c, w_mod, b_mod):
    depth, d, n = w_mod.shape
    return pl.pallas_call(
        _mod_kernel,
        grid=(depth, n // MOD_TN),
        in_specs=[pl.BlockSpec((d, 1), lambda l, j: (0, 0)),
                  pl.BlockSpec((1, d, MOD_TN), lambda l, j: (l, 0, j)),
                  pl.BlockSpec((1, 1, MOD_TN), lambda l, j: (l, 0, j))],
        out_specs=pl.BlockSpec((1, 1, MOD_TN), lambda l, j: (l, 0, j)),
        out_shape=jax.ShapeDtypeStruct((depth, 1, n), F32),
        scratch_shapes=[pltpu.VMEM((d, 128), F32)],
        compiler_params=_params(("arbitrary", "arbitrary"), 40),
        name="mod_matvec",
    )(c.reshape(d, 1), w_mod, b_mod.reshape(depth, 1, n))


def _modulate_kernel(x_ref, sc_ref, sh_ref, u_ref):
    u_ref[...] = (x_ref[...] * (1.0 + sc_ref[...]) + sh_ref[...]).astype(u_ref.dtype)


def _modulate(x, scale, shift):
    s, d = x.shape
    row = pl.BlockSpec((1, d), lambda i: (0, 0))
    return pl.pallas_call(
        _modulate_kernel,
        grid=(s // LN_TM,),
        in_specs=[pl.BlockSpec((LN_TM, d), lambda i: (i, 0)), row, row],
        out_specs=pl.BlockSpec((LN_TM, d), lambda i: (i, 0)),
        out_shape=jax.ShapeDtypeStruct((s, d), BF16),
        compiler_params=_params(("arbitrary",), 32),
        name="modulate",
    )(x, scale, shift)


def _ln_kernel(x_ref, y_ref, gate_ref, g_ref, b_ref, *rest, with_next):
    z = DEEPNORM_ALPHA * x_ref[...] + (1.0 + gate_ref[...]) * y_ref[...]
    mu = jnp.mean(z, axis=-1, keepdims=True)
    zc = z - mu
    var = jnp.mean(zc * zc, axis=-1, keepdims=True)
    xn = zc * lax.rsqrt(var + LN_EPS) * g_ref[...] + b_ref[...]
    if with_next:
        sc_ref, sh_ref, xo_ref, uo_ref = rest
        xo_ref[...] = xn
        uo_ref[...] = (xn * (1.0 + sc_ref[...]) + sh_ref[...]).astype(uo_ref.dtype)
    else:
        (xo_ref,) = rest
        xo_ref[...] = xn


def _residual_ln(x, y, gate, g, b, nxt=None):
    s, d = x.shape
    row = pl.BlockSpec((1, d), lambda i: (0, 0))
    tile = pl.BlockSpec((LN_TM, d), lambda i: (i, 0))
    ins = [x, y, gate, g, b]
    in_specs = [tile, tile, row, row, row]
    out_shape = [jax.ShapeDtypeStruct((s, d), F32)]
    out_specs = [tile]
    if nxt is not None:
        ins += list(nxt)
        in_specs += [row, row]
        out_shape.append(jax.ShapeDtypeStruct((s, d), BF16))
        out_specs.append(tile)
    res = pl.pallas_call(
        functools.partial(_ln_kernel, with_next=nxt is not None),
        grid=(s // LN_TM,),
        in_specs=in_specs, out_specs=out_specs, out_shape=out_shape,
        compiler_params=_params(("arbitrary",), 48),
        name="residual_ln",
    )(*ins)
    return res if nxt is not None else (res[0], None)


def _mm_kernel(*refs, n_lhs):
    lhs = refs[:n_lhs]
    w_ref, o_ref, wb_ref = refs[n_lhs:]

    @pl.when(pl.program_id(1) == 0)
    def _():
        wb_ref[...] = w_ref[...].astype(wb_ref.dtype)

    acc = None
    k0 = 0
    for a in lhs:
        kk = a.shape[1]
        part = jnp.dot(a[...], wb_ref[k0:k0 + kk, :], preferred_element_type=F32)
        acc = part if acc is None else acc + part
        k0 += kk
    o_ref[...] = acc.astype(o_ref.dtype)


def _mm_dilated_kernel(u_ref, w_ref, o1_ref, o4_ref, o16_ref, wb_ref, acc_ref):
    @pl.when(pl.program_id(1) == 0)
    def _():
        wb_ref[...] = w_ref[...].astype(wb_ref.dtype)

    acc = jnp.dot(u_ref[...], wb_ref[...], preferred_element_type=F32)
    o1_ref[...] = acc.astype(o1_ref.dtype)
    n_cb, tm, _ = acc_ref.shape
    for cb in range(n_cb):
        acc_ref[cb] = acc[:, cb * 128:(cb + 1) * 128]
    for o_ref, d in ((o4_ref, DILATIONS[1]), (o16_ref, DILATIONS[2])):
        for r in range(d):
            for cb in range(n_cb):
                o_ref[r, :, cb * 128:(cb + 1) * 128] = (
                    acc_ref[cb, pl.ds(r, tm // d, stride=d), :].astype(o_ref.dtype))


def _weight_spec(w, layer, col_block0):
    k = w.shape[-2]
    if w.ndim == 3:
        return pl.BlockSpec((None, k, MM_TN), lambda j, i: (layer, 0, col_block0 + j))
    return pl.BlockSpec((k, MM_TN), lambda j, i: (0, col_block0 + j))


def _in_proj_dilated(u, w, layer, n_cols):
    s, k = u.shape
    d4, d16 = DILATIONS[1], DILATIONS[2]
    return pl.pallas_call(
        _mm_dilated_kernel,
        grid=(n_cols // MM_TN, s // MM_TM),
        in_specs=[pl.BlockSpec((MM_TM, k), lambda j, i: (i, 0)), _weight_spec(w, layer, 0)],
        out_specs=[pl.BlockSpec((MM_TM, MM_TN), lambda j, i: (i, j)),
                   pl.BlockSpec((d4, MM_TM // d4, MM_TN), lambda j, i: (0, i, j)),
                   pl.BlockSpec((d16, MM_TM // d16, MM_TN), lambda j, i: (0, i, j))],
        out_shape=[jax.ShapeDtypeStruct((s, n_cols), BF16),
                   jax.ShapeDtypeStruct((d4, s // d4, n_cols), BF16),
                   jax.ShapeDtypeStruct((d16, s // d16, n_cols), BF16)],
        scratch_shapes=[pltpu.VMEM((k, MM_TN), BF16),
                        pltpu.VMEM((MM_TN // 128, MM_TM, 128), F32)],
        compiler_params=_params(("arbitrary", "arbitrary"), 56),
        name="in_proj_dilated",
    )(u, w)


def _matmul(lhs, w, layer, col_block0, n_cols, out_dtype, name):
    s = lhs[0].shape[0]
    k = w.shape[-2]
    assert sum(a.shape[1] for a in lhs) == k
    return pl.pallas_call(
        functools.partial(_mm_kernel, n_lhs=len(lhs)),
        grid=(n_cols // MM_TN, s // MM_TM),
        in_specs=[pl.BlockSpec((MM_TM, a.shape[1]), lambda j, i: (i, 0)) for a in lhs]
        + [_weight_spec(w, layer, col_block0)],
        out_specs=pl.BlockSpec((MM_TM, MM_TN), lambda j, i: (i, j)),
        out_shape=jax.ShapeDtypeStruct((s, n_cols), out_dtype),
        scratch_shapes=[pltpu.VMEM((k, MM_TN), BF16)],
        compiler_params=_params(("arbitrary", "arbitrary"), 56),
        name=name,
    )(*lhs, w)


def _band_bias(slope, dist_scale, max_dist):
    qi = lax.broadcasted_iota(jnp.int32, (BLOCK, 2 * BLOCK), 0)
    kj = lax.broadcasted_iota(jnp.int32, (BLOCK, 2 * BLOCK), 1)
    dist = qi + BLOCK - kj
    band = (dist >= 0) & (dist <= max_dist)
    bias = jnp.where(band, -(slope * dist_scale) * dist.astype(F32), NEG)
    return bias, jnp.where(kj >= BLOCK, bias, NEG)


def _scores(q, kk, scale, bias):
    s = lax.dot_general(q, kk, (((1,), (1,)), ((), ())), preferred_element_type=F32)
    return s * scale + bias


def _attn_a_kernel(slope_ref,
                   q1_ref, k1_ref, v1_ref, kp1_ref, vp1_ref,
                   q4_ref, k4_ref, v4_ref, kp4_ref, vp4_ref,
                   q16_ref, k16_ref, v16_ref, kp16_ref, vp16_ref,
                   o_ref,
                   kb1, vb1, kb4, vb4, kb16, vb16, bias_ref, acc_ref, lse_ref, nat_ref, lsen_ref):
    for hh in range(ATTN_HEADS):
        _attn_a_head(slope_ref[pl.program_id(0) * ATTN_HEADS + hh], pl.program_id(1),
                     slice(hh * HEAD_DIM_A, (hh + 1) * HEAD_DIM_A),
                     q1_ref, k1_ref, v1_ref, kp1_ref, vp1_ref,
                     q4_ref, k4_ref, v4_ref, kp4_ref, vp4_ref,
                     q16_ref, k16_ref, v16_ref, kp16_ref, vp16_ref, o_ref,
                     kb1, vb1, kb4, vb4, kb16, vb16, bias_ref, acc_ref, lse_ref, nat_ref, lsen_ref)


def _attn_a_head(slope, sp, cs,
                 q1_ref, k1_ref, v1_ref, kp1_ref, vp1_ref,
                 q4_ref, k4_ref, v4_ref, kp4_ref, vp4_ref,
                 q16_ref, k16_ref, v16_ref, kp16_ref, vp16_ref, o_ref,
                 kb1, vb1, kb4, vb4, kb16, vb16, bias_ref, acc_ref, lse_ref, nat_ref, lsen_ref):
    scale = HEAD_DIM_A ** -0.5
    d4, d16 = DILATIONS[1], DILATIONS[2]
    n4 = SPAN // d4

    for bi, d in enumerate(DILATIONS):
        b0, b1 = _band_bias(slope, float(d), MAX_DIST_A)
        bias_ref[2 * bi] = b0
        bias_ref[2 * bi + 1] = b1

    kb1[0:BLOCK] = kp1_ref[:, cs]
    kb1[BLOCK:] = k1_ref[:, cs]
    vb1[0:BLOCK] = vp1_ref[:, cs]
    vb1[BLOCK:] = v1_ref[:, cs]
    kb4[:, 0:BLOCK] = kp4_ref[:, :, cs]
    kb4[:, BLOCK:] = k4_ref[:, :, cs]
    vb4[:, 0:BLOCK] = vp4_ref[:, :, cs]
    vb4[:, BLOCK:] = v4_ref[:, :, cs]
    kb16[:, 0:BLOCK] = kp16_ref[:, :, cs]
    kb16[:, BLOCK:] = k16_ref[:, :, cs]
    vb16[:, 0:BLOCK] = vp16_ref[:, :, cs]
    vb16[:, BLOCK:] = v16_ref[:, :, cs]

    def block(q, kk, vv, bias):
        s = _scores(q, kk, scale, bias)
        m = jnp.max(s, axis=-1, keepdims=True)
        p = jnp.exp(s - m)
        l = jnp.sum(p, axis=-1, keepdims=True)
        o = jnp.dot(p.astype(BF16), vv, preferred_element_type=F32) * (1.0 / l)
        return o, jnp.broadcast_to(m + jnp.log(l), o.shape)

    def body1(b, carry):
        r0 = pl.multiple_of(b * BLOCK, BLOCK)
        first = ((sp == 0) & (b == 0)).astype(jnp.int32)
        o, lse = block(q1_ref[pl.ds(r0, BLOCK), cs], kb1[pl.ds(r0, 2 * BLOCK), :],
                       vb1[pl.ds(r0, 2 * BLOCK), :], bias_ref[first])
        acc_ref[0, pl.ds(r0, BLOCK), :] = o
        lse_ref[0, pl.ds(r0, BLOCK), :] = lse
        return carry

    lax.fori_loop(0, SPAN // BLOCK, body1, 0, unroll=ATTN_UNROLL)

    def body4(idx, carry):
        r = idx // (n4 // BLOCK)
        b = idx % (n4 // BLOCK)
        r0 = pl.multiple_of(b * BLOCK, BLOCK)
        first = ((sp == 0) & (b == 0)).astype(jnp.int32)
        o, lse = block(q4_ref[r, pl.ds(r0, BLOCK), cs], kb4[r, pl.ds(r0, 2 * BLOCK), :],
                       vb4[r, pl.ds(r0, 2 * BLOCK), :], bias_ref[2 + first])
        w0 = pl.multiple_of(idx * BLOCK, BLOCK)
        acc_ref[1, pl.ds(w0, BLOCK), :] = o
        lse_ref[1, pl.ds(w0, BLOCK), :] = lse
        return carry

    lax.fori_loop(0, SPAN // BLOCK, body4, 0, unroll=ATTN_UNROLL)

    def body16(r, carry):
        first = (sp == 0).astype(jnp.int32)
        o, lse = block(q16_ref[r, :, cs], kb16[r], vb16[r], bias_ref[4 + first])
        w0 = pl.multiple_of(r * BLOCK, BLOCK)
        acc_ref[2, pl.ds(w0, BLOCK), :] = o
        lse_ref[2, pl.ds(w0, BLOCK), :] = lse
        return carry

    lax.fori_loop(0, d16, body16, 0, unroll=ATTN_UNROLL)

    for r in range(d4):
        nat_ref[0, pl.ds(r, n4, stride=d4), :] = acc_ref[1, r * n4:(r + 1) * n4, :]
        lsen_ref[0, pl.ds(r, n4, stride=d4), :] = lse_ref[1, r * n4:(r + 1) * n4, :]
    for r in range(d16):
        nat_ref[1, pl.ds(r, BLOCK, stride=d16), :] = acc_ref[2, r * BLOCK:(r + 1) * BLOCK, :]
        lsen_ref[1, pl.ds(r, BLOCK, stride=d16), :] = lse_ref[2, r * BLOCK:(r + 1) * BLOCK, :]

    chunk = 256

    def merge(c, carry):
        r0 = pl.multiple_of(c * chunk, chunk)
        rows = pl.ds(r0, chunk)
        l1 = lse_ref[0, rows, :]
        l4 = lsen_ref[0, rows, :]
        l16 = lsen_ref[1, rows, :]
        mx = jnp.maximum(jnp.maximum(l1, l4), l16)
        e1 = jnp.exp(l1 - mx)
        e4 = jnp.exp(l4 - mx)
        e16 = jnp.exp(l16 - mx)
        inv = 1.0 / (e1 + e4 + e16)
        o = ((e1 * inv) * acc_ref[0, rows, :] + (e4 * inv) * nat_ref[0, rows, :]
             + (e16 * inv) * nat_ref[1, rows, :])
        o_ref[rows, cs] = o.astype(o_ref.dtype)
        return carry

    lax.fori_loop(0, SPAN // chunk, merge, 0)


def _attn_a(h1, h4, h16, slopes_a):
    s = h1.shape[0]
    d4, d16 = DILATIONS[1], DILATIONS[2]
    n4 = SPAN // d4
    nh = N_HEADS_A // ATTN_HEADS
    hw = ATTN_HEADS * HEAD_DIM_A
    blk1 = SPAN // BLOCK

    def cur1(off):
        return pl.BlockSpec((SPAN, hw), lambda h, sp: (sp, off + h))

    def prev1(off):
        return pl.BlockSpec((BLOCK, hw), lambda h, sp: (jnp.maximum(sp * blk1 - 1, 0), off + h))

    def cur4(off):
        return pl.BlockSpec((d4, n4, hw), lambda h, sp: (0, sp, off + h))

    def prev4(off):
        return pl.BlockSpec((d4, BLOCK, hw),
                            lambda h, sp: (0, jnp.maximum(sp * (n4 // BLOCK) - 1, 0), off + h))

    def cur16(off):
        return pl.BlockSpec((d16, BLOCK, hw), lambda h, sp: (0, sp, off + h))

    def prev16(off):
        return pl.BlockSpec((d16, BLOCK, hw), lambda h, sp: (0, jnp.maximum(sp - 1, 0), off + h))

    in_specs = [pl.BlockSpec(memory_space=pltpu.SMEM)]
    args = [slopes_a]
    for arr, cur, prev in ((h1, cur1, prev1), (h4, cur4, prev4), (h16, cur16, prev16)):
        in_specs += [cur(0), cur(nh), cur(2 * nh), prev(nh), prev(2 * nh)]
        args += [arr] * 5
    return pl.pallas_call(
        _attn_a_kernel,
        grid=(nh, s // SPAN),
        in_specs=in_specs,
        out_specs=pl.BlockSpec((SPAN, hw), lambda h, sp: (sp, h)),
        out_shape=jax.ShapeDtypeStruct((s, WIDTH_A), BF16),
        scratch_shapes=[
            pltpu.VMEM((BLOCK + SPAN, HEAD_DIM_A), BF16), pltpu.VMEM((BLOCK + SPAN, HEAD_DIM_A), BF16),
            pltpu.VMEM((d4, BLOCK + n4, HEAD_DIM_A), BF16), pltpu.VMEM((d4, BLOCK + n4, HEAD_DIM_A), BF16),
            pltpu.VMEM((d16, 2 * BLOCK, HEAD_DIM_A), BF16), pltpu.VMEM((d16, 2 * BLOCK, HEAD_DIM_A), BF16),
            pltpu.VMEM((6, BLOCK, 2 * BLOCK), F32),
            pltpu.VMEM((3, SPAN, HEAD_DIM_A), F32),
            pltpu.VMEM((3, SPAN, HEAD_DIM_A), F32),
            pltpu.VMEM((2, SPAN, HEAD_DIM_A), F32),
            pltpu.VMEM((2, SPAN, HEAD_DIM_A), F32),
        ],
        compiler_params=_params(("arbitrary", "arbitrary"), 48),
        name="attn_dilated",
    )(*args)


def _attn_b_kernel(slope_ref, sink_ref, q_ref, k_ref, v_ref, kp_ref, vp_ref, o_ref, kb, vb, bias_ref):
    hk = pl.program_id(0)
    sp = pl.program_id(1)
    scale = HEAD_DIM_B ** -0.5

    for g in range(GQA_GROUP):
        b0, b1 = _band_bias(slope_ref[hk * GQA_GROUP + g], 1.0, MAX_DIST_B)
        bias_ref[2 * g] = b0
        bias_ref[2 * g + 1] = b1

    src = lax.broadcasted_iota(jnp.int32, (128, 128), 0)
    dst = lax.broadcasted_iota(jnp.int32, (128, 128), 1)
    pick = (src == (hk % 2) * HEAD_DIM_B + dst % HEAD_DIM_B).astype(BF16)

    def dup(ref):
        return jnp.dot(ref[...], pick, preferred_element_type=F32).astype(BF16)

    kb[0:BLOCK] = dup(kp_ref)
    kb[BLOCK:] = dup(k_ref)
    vb[0:BLOCK] = dup(vp_ref)
    vb[BLOCK:] = dup(v_ref)

    lane = lax.broadcasted_iota(jnp.int32, (BLOCK, 2 * HEAD_DIM_B), 1)
    low = lane < HEAD_DIM_B

    def body(b, carry):
        r0 = pl.multiple_of(b * BLOCK, BLOCK)
        first = ((sp == 0) & (b == 0)).astype(jnp.int32)
        kk = kb[pl.ds(r0, 2 * BLOCK), :]
        vv = vb[pl.ds(r0, 2 * BLOCK), :]
        for pr in range(GQA_GROUP // 2):
            q2 = q_ref[pl.ds(r0, BLOCK), pr * 128:(pr + 1) * 128]
            outs = []
            for half in range(2):
                g = 2 * pr + half
                keep = low if half == 0 else jnp.logical_not(low)
                qq = jnp.where(keep, q2, jnp.zeros_like(q2))
                s = _scores(qq, kk, scale, bias_ref[2 * g + first])
                sink = sink_ref[hk * GQA_GROUP + g]
                m = jnp.maximum(jnp.max(s, axis=-1, keepdims=True), sink)
                p = jnp.exp(s - m)
                l = jnp.sum(p, axis=-1, keepdims=True) + jnp.exp(sink - m)
                outs.append(jnp.dot(p.astype(BF16), vv, preferred_element_type=F32) * (1.0 / l))
            o_ref[pl.ds(r0, BLOCK), pr * 128:(pr + 1) * 128] = jnp.where(low, outs[0], outs[1]).astype(o_ref.dtype)
        return carry

    lax.fori_loop(0, SPAN // BLOCK, body, 0, unroll=4)


def _attn_b(hb, slopes_b, sinks):
    s = hb.shape[0]
    gw = GQA_GROUP * HEAD_DIM_B
    blk = SPAN // BLOCK
    k_off = WIDTH_B // 128
    v_off = k_off + KV_WIDTH_B // 128

    def cur(off):
        return pl.BlockSpec((SPAN, 128), lambda h, sp: (sp, off + h // 2))

    def prev(off):
        return pl.BlockSpec((BLOCK, 128), lambda h, sp: (jnp.maximum(sp * blk - 1, 0), off + h // 2))

    smem = pl.BlockSpec(memory_space=pltpu.SMEM)
    return pl.pallas_call(
        _attn_b_kernel,
        grid=(N_KV_HEADS_B, s // SPAN),
        in_specs=[smem, smem, pl.BlockSpec((SPAN, gw), lambda h, sp: (sp, h)),
                  cur(k_off), cur(v_off), prev(k_off), prev(v_off)],
        out_specs=pl.BlockSpec((SPAN, gw), lambda h, sp: (sp, h)),
        out_shape=jax.ShapeDtypeStruct((s, WIDTH_B), BF16),
        scratch_shapes=[pltpu.VMEM((BLOCK + SPAN, 128), BF16), pltpu.VMEM((BLOCK + SPAN, 128), BF16),
                        pltpu.VMEM((2 * GQA_GROUP, BLOCK, 2 * BLOCK), F32)],
        compiler_params=_params(("arbitrary", "arbitrary"), 48),
        name="attn_swa",
    )(slopes_b, sinks, hb, hb, hb, hb, hb)


def _mlp_kernel(u_ref, wg_ref, wv_ref, cwg_ref, cwv_ref, cbg_ref, cbv_ref, wd_ref, o_ref,
                hb0, hb1, act0, act1, carry, *, nj, last):
    t = pl.program_id(0)
    tm = u_ref.shape[0]
    tn = wg_ref.shape[1]
    tu = jnp.minimum(t, last)
    i_u = tu // nj
    j_u = tu % nj
    j_d = jnp.maximum(t - 1, 0) % nj

    @pl.when(t == 0)
    def _():
        act0[...] = jnp.zeros(act0.shape, BF16)

    @pl.when(j_d == 0)
    def _():
        o_ref[...] = jnp.zeros(o_ref.shape, F32)

    def step(act_w, act_r):
        u = u_ref[...]
        for k, hb in enumerate((hb0, hb1)):
            cols = slice(k * MLP_COLS, (k + 1) * MLP_COLS)
            hb[8:, 0:MLP_COLS] = jnp.dot(u, wg_ref[:, cols], preferred_element_type=F32)
            hb[8:, MLP_COLS:] = jnp.dot(u, wv_ref[:, cols], preferred_element_type=F32)
            hb[0:8, :] = jnp.where(i_u == 0, 0.0, carry[j_u, k])
            carry[j_u, k] = hb[tm:tm + 8, :]

            def conv(c0, cw_ref, cb_ref):
                acc = cw_ref[0:1, cols] * hb[8 - 2:8 - 2 + tm, c0:c0 + MLP_COLS]
                acc = acc + cw_ref[1:2, cols] * hb[8 - 1:8 - 1 + tm, c0:c0 + MLP_COLS]
                acc = acc + cw_ref[2:3, cols] * hb[8:8 + tm, c0:c0 + MLP_COLS]
                return acc + cb_ref[:, cols]

            gate = conv(0, cwg_ref, cbg_ref)
            val = conv(MLP_COLS, cwv_ref, cbv_ref)
            act_w[:, cols] = (gate * jax.nn.sigmoid(gate) * val).astype(BF16)
        o_ref[...] += jnp.dot(act_r[...], wd_ref[...], preferred_element_type=F32)

    @pl.when(t % 2 == 0)
    def _():
        step(act1, act0)

    @pl.when(t % 2 == 1)
    def _():
        step(act0, act1)


def _conv_ffn(u, w_up, conv_w, conv_b, w_down):
    s, d = u.shape
    nj = FF_PAD // MLP_TN
    n_chunks = (s // MLP_TM) * nj
    last = n_chunks - 1

    def chunk(lag):
        def ij(t):
            c = jnp.clip(t - lag, 0, last)
            return c // nj, c % nj
        return ij

    up, dn = chunk(0), chunk(1)
    return pl.pallas_call(
        functools.partial(_mlp_kernel, nj=nj, last=last),
        grid=(n_chunks + 1,),
        in_specs=[pl.BlockSpec((MLP_TM, d), lambda t: (up(t)[0], 0)),
                  pl.BlockSpec((d, MLP_TN), lambda t: (0, up(t)[1])),
                  pl.BlockSpec((d, MLP_TN), lambda t: (0, nj + up(t)[1])),
                  pl.BlockSpec((CONV_WIDTH, MLP_TN), lambda t: (0, up(t)[1])),
                  pl.BlockSpec((CONV_WIDTH, MLP_TN), lambda t: (0, nj + up(t)[1])),
                  pl.BlockSpec((1, MLP_TN), lambda t: (0, up(t)[1])),
                  pl.BlockSpec((1, MLP_TN), lambda t: (0, nj + up(t)[1])),
                  pl.BlockSpec((MLP_TN, d), lambda t: (dn(t)[1], 0))],
        out_specs=pl.BlockSpec((MLP_TM, d), lambda t: (dn(t)[0], 0)),
        out_shape=jax.ShapeDtypeStruct((s, d), F32),
        scratch_shapes=[pltpu.VMEM((8 + MLP_TM, 2 * MLP_COLS), F32),
                        pltpu.VMEM((8 + MLP_TM, 2 * MLP_COLS), F32),
                        pltpu.VMEM((MLP_TM, MLP_TN), BF16),
                        pltpu.VMEM((MLP_TM, MLP_TN), BF16),
                        pltpu.VMEM((nj, MLP_TN // MLP_COLS, 8, 2 * MLP_COLS), F32)],
        compiler_params=_params(("arbitrary",), 60),
        name="conv_ffn",
    )(u, w_up, w_up, conv_w, conv_w, conv_b, conv_b, w_down)


def _pad_ff(a, axis):
    pad = [(0, 0)] * a.ndim
    pad[axis] = (0, FF_PAD - D_FF)
    g, v = jnp.split(a, 2, axis=axis)
    return jnp.concatenate([jnp.pad(g, pad), jnp.pad(v, pad)], axis=axis)


def _cast_pad_kernel(w_ref, o_ref, *, axis, n_valid):
    c = pl.program_id(axis)

    @pl.when(c < n_valid)
    def _():
        o_ref[...] = w_ref[...].astype(o_ref.dtype)

    @pl.when(c >= n_valid)
    def _():
        o_ref[...] = jnp.zeros(o_ref.shape, o_ref.dtype)


def _cast_w_up(w_up, layer):
    d = w_up.shape[1]
    nv, npad = D_FF // CAST_BLK, FF_PAD // CAST_BLK
    return pl.pallas_call(
        functools.partial(_cast_pad_kernel, axis=1, n_valid=nv),
        grid=(2, npad),
        in_specs=[pl.BlockSpec((None, d, CAST_BLK),
                               lambda h, c: (layer, 0, h * nv + jnp.minimum(c, nv - 1)))],
        out_specs=pl.BlockSpec((d, CAST_BLK), lambda h, c: (0, h * npad + c)),
        out_shape=jax.ShapeDtypeStruct((d, 2 * FF_PAD), BF16),
        compiler_params=_params(("arbitrary", "arbitrary"), 32),
        name="cast_w_up",
    )(w_up)


def _cast_w_down(w_down, layer):
    d = w_down.shape[2]
    nv, npad = D_FF // CAST_BLK, FF_PAD // CAST_BLK
    return pl.pallas_call(
        functools.partial(_cast_pad_kernel, axis=0, n_valid=nv),
        grid=(npad,),
        in_specs=[pl.BlockSpec((None, CAST_BLK, d), lambda c: (layer, jnp.minimum(c, nv - 1), 0))],
        out_specs=pl.BlockSpec((CAST_BLK, d), lambda c: (c, 0)),
        out_shape=jax.ShapeDtypeStruct((FF_PAD, d), BF16),
        compiler_params=_params(("arbitrary",), 32),
        name="cast_w_down",
    )(w_down)


def _alibi_slopes():
    i = jnp.arange(1, N_ALIBI_HEADS + 1, dtype=F32)
    return jnp.exp2(-8.0 * i / N_ALIBI_HEADS)


def kernel(x, c, w_mod, b_mod, w_in, sinks, w_out, ln1_g, ln1_b, w_up, conv_w, conv_b, w_down, ln2_g, ln2_b):
    b, s, d = x.shape
    assert (b, s, d) == (1, SEQ, D_MODEL)
    slopes = _alibi_slopes()
    slopes_b, slopes_a = slopes[:N_Q_HEADS_B], slopes[N_Q_HEADS_B:]
    mod = _mod_all(c, w_mod, b_mod)
    x = x.reshape(s, d)
    u = None
    for l in range(DEPTH):
        shift_a, scale_a, gate_a, shift_m, scale_m, gate_m = (
            mod[l, :, n * d:(n + 1) * d] for n in range(N_MOD))
        w_up_l = _cast_w_up(w_up, l)
        w_down_l = _cast_w_down(w_down, l)
        conv_w_l = _pad_ff(conv_w[l], 1)
        conv_b_l = _pad_ff(conv_b[l].reshape(1, 2 * D_FF), 1)

        if u is None:
            u = _modulate(x, scale_a, shift_a)
        h1, h4, h16 = _in_proj_dilated(u, w_in, l, QKV_A)
        hb = _matmul([u], w_in, l, QKV_A // MM_TN, WIDTH_B + 2 * KV_WIDTH_B, BF16, "in_proj_b")
        mixed_a = _attn_a(h1, h4, h16, slopes_a)
        mixed_b = _attn_b(hb, slopes_b, sinks[l])
        att = _matmul([mixed_a, mixed_b], w_out, l, 0, d, F32, "out_proj")
        x, u = _residual_ln(x, att, gate_a, ln1_g[l].reshape(1, d), ln1_b[l].reshape(1, d),
                            (scale_m, shift_m))
        y = _conv_ffn(u, w_up_l, conv_w_l, conv_b_l, w_down_l)
        nxt = None
        if l + 1 < DEPTH:
            nxt = (mod[l + 1, :, d:2 * d], mod[l + 1, :, 0:d])
        x, u = _residual_ln(x, y, gate_m, ln2_g[l].reshape(1, d), ln2_b[l].reshape(1, d), nxt)
    return x.reshape(b, s, d)
```

```python
import functools
import math

import jax
import jax.numpy as jnp
from jax import lax
from jax.experimental import pallas as pl
from jax.experimental.pallas import tpu as pltpu

F32 = jnp.float32
BF16 = jnp.bfloat16

D_MODEL = 4096
SEQ = 8192
DEPTH = 2
HEAD_DIM_A = 128
N_HEADS_A = 16
DILATIONS = (1, 4, 16)
MAX_DIST_A = 128
HEAD_DIM_B = 64
N_Q_HEADS_B = 32
N_KV_HEADS_B = 4
GQA_GROUP = 8
MAX_DIST_B = 127
WIDTH_A = N_HEADS_A * HEAD_DIM_A
WIDTH_B = N_Q_HEADS_B * HEAD_DIM_B
KV_WIDTH_B = N_KV_HEADS_B * HEAD_DIM_B
QKV_A = 3 * WIDTH_A
N_ALIBI_HEADS = N_HEADS_A + N_Q_HEADS_B
D_FF = 11008
CONV_WIDTH = 3
N_MOD = 6
DEEPNORM_ALPHA = (2 * DEPTH) ** 0.25
LN_EPS = 1e-5
BLOCK = 128
NEG = -1e30

SPAN = 2048
MM_TM = 1024
MM_TN = 512
MLP_TM = 512
MLP_TN = 512
FF_PAD = MLP_TN * math.ceil(D_FF / MLP_TN)
LN_TM = 256
CAST_BLK = 256
ATTN_HEADS = 1
ATTN_UNROLL = 16
MOD_TN = 512
MIB = 1024 * 1024


def _params(sem, vmem_mib):
    return pltpu.CompilerParams(dimension_semantics=sem, vmem_limit_bytes=vmem_mib * MIB)


def _mod_kernel(c_ref, w_ref, b_ref, o_ref, cb_ref):
    @pl.when((pl.program_id(0) == 0) & (pl.program_id(1) == 0))
    def _():
        cc = c_ref[...]
        cb_ref[...] = jnp.broadcast_to(cc * jax.nn.sigmoid(cc), cb_ref.shape)

    k_dim = w_ref.shape[1]
    n_grp = w_ref.shape[2] // 128

    def body(k, accs):
        r = pl.multiple_of(k * 8, 8)
        cb = cb_ref[pl.ds(r, 8), :]
        return tuple(a + w_ref[0, pl.ds(r, 8), g * 128:(g + 1) * 128] * cb
                     for g, a in enumerate(accs))

    accs = lax.fori_loop(0, k_dim // 8, body,
                         tuple(jnp.zeros((8, 128), F32) for _ in range(n_grp)), unroll=8)
    for g, a in enumerate(accs):
        o_ref[0, :, g * 128:(g + 1) * 128] = (jnp.sum(a, axis=0, keepdims=True)
                                              + b_ref[0, :, g * 128:(g + 1) * 128])


def _mod_all(c, w_mod, b_mod):
    depth, d, n = w_mod.shape
    return pl.pallas_call(
        _mod_kernel,
        grid=(depth, n // MOD_TN),
        in_specs=[pl.BlockSpec((d, 1), lambda l, j: (0, 0)),
                  pl.BlockSpec((1, d, MOD_TN), lambda l, j: (l, 0, j)),
                  pl.BlockSpec((1, 1, MOD_TN), lambda l, j: (l, 0, j))],
        out_specs=pl.BlockSpec((1, 1, MOD_TN), lambda l, j: (l, 0, j)),
        out_shape=jax.ShapeDtypeStruct((depth, 1, n), F32),
        scratch_shapes=[pltpu.VMEM((d, 128), F32)],
        compiler_params=_params(("arbitrary", "arbitrary"), 40),
        name="mod_matvec",
    )(c.reshape(d, 1), w_mod, b_mod.reshape(depth, 1, n))


def _modulate_kernel(x_ref, sc_ref, sh_ref, u_ref):
    u_ref[...] = (x_ref[...] * (1.0 + sc_ref[...]) + sh_ref[...]).astype(u_ref.dtype)


def _modulate(x, scale, shift):
    s, d = x.shape
    row = pl.BlockSpec((1, d), lambda i: (0, 0))
    return pl.pallas_call(
        _modulate_kernel,
        grid=(s // LN_TM,),
        in_specs=[pl.BlockSpec((LN_TM, d), lambda i: (i, 0)), row, row],
        out_specs=pl.BlockSpec((LN_TM, d), lambda i: (i, 0)),
        out_shape=jax.ShapeDtypeStruct((s, d), BF16),
        compiler_params=_params(("arbitrary",), 32),
        name="modulate",
    )(x, scale, shift)


def _ln_kernel(x_ref, y_ref, gate_ref, g_ref, b_ref, *rest, with_next):
    z = DEEPNORM_ALPHA * x_ref[...] + (1.0 + gate_ref[...]) * y_ref[...]
    mu = jnp.mean(z, axis=-1, keepdims=True)
    zc = z - mu
    var = jnp.mean(zc * zc, axis=-1, keepdims=True)
    xn = zc * lax.rsqrt(var + LN_EPS) * g_ref[...] + b_ref[...]
    if with_next:
        sc_ref, sh_ref, xo_ref, uo_ref = rest
        xo_ref[...] = xn
        uo_ref[...] = (xn * (1.0 + sc_ref[...]) + sh_ref[...]).astype(uo_ref.dtype)
    else:
        (xo_ref,) = rest
        xo_ref[...] = xn


def _residual_ln(x, y, gate, g, b, nxt=None):
    s, d = x.shape
    row = pl.BlockSpec((1, d), lambda i: (0, 0))
    tile = pl.BlockSpec((LN_TM, d), lambda i: (i, 0))
    ins = [x, y, gate, g, b]
    in_specs = [tile, tile, row, row, row]
    out_shape = [jax.ShapeDtypeStruct((s, d), F32)]
    out_specs = [tile]
    if nxt is not None:
        ins += list(nxt)
        in_specs += [row, row]
        out_shape.append(jax.ShapeDtypeStruct((s, d), BF16))
        out_specs.append(tile)
    res = pl.pallas_call(
        functools.partial(_ln_kernel, with_next=nxt is not None),
        grid=(s // LN_TM,),
        in_specs=in_specs, out_specs=out_specs, out_shape=out_shape,
        compiler_params=_params(("arbitrary",), 48),
        name="residual_ln",
    )(*ins)
    return res if nxt is not None else (res[0], None)


def _mm_kernel(*refs, n_lhs):
    lhs = refs[:n_lhs]
    w_ref, o_ref, wb_ref = refs[n_lhs:]

    @pl.when(pl.program_id(1) == 0)
    def _():
        wb_ref[...] = w_ref[...].astype(wb_ref.dtype)

    acc = None
    k0 = 0
    for a in lhs:
        kk = a.shape[1]
        part = jnp.dot(a[...], wb_ref[k0:k0 + kk, :], preferred_element_type=F32)
        acc = part if acc is None else acc + part
        k0 += kk
    o_ref[...] = acc.astype(o_ref.dtype)


def _mm_dilated_kernel(u_ref, w_ref, o1_ref, o4_ref, o16_ref, wb_ref, acc_ref, s4_ref):
    @pl.when(pl.program_id(1) == 0)
    def _():
        wb_ref[...] = w_ref[...].astype(wb_ref.dtype)

    n_cb, tm, _ = acc_ref.shape
    d4 = DILATIONS[1]
    assert DILATIONS[2] == d4 * d4
    n4 = tm // d4
    acc = jnp.dot(u_ref[...], wb_ref[...], preferred_element_type=F32)
    o1_ref[...] = acc.astype(o1_ref.dtype)
    for cb in range(n_cb):
        cols = slice(cb * 128, (cb + 1) * 128)
        acc_ref[cb] = acc[:, cols]
        for r in range(d4):
            rows4 = acc_ref[cb, pl.ds(r, n4, stride=d4), :]
            o4_ref[r, :, cols] = rows4.astype(o4_ref.dtype)
            s4_ref[cb, r] = rows4
        for r in range(d4):
            for rr in range(d4):
                o16_ref[r + d4 * rr, :, cols] = (
                    s4_ref[cb, r, pl.ds(rr, n4 // d4, stride=d4), :].astype(o16_ref.dtype))


def _weight_spec(w, layer, col_block0):
    k = w.shape[-2]
    if w.ndim == 3:
        return pl.BlockSpec((None, k, MM_TN), lambda j, i: (layer, 0, col_block0 + j))
    return pl.BlockSpec((k, MM_TN), lambda j, i: (0, col_block0 + j))


def _in_proj_dilated(u, w, layer, n_cols):
    s, k = u.shape
    d4, d16 = DILATIONS[1], DILATIONS[2]
    return pl.pallas_call(
        _mm_dilated_kernel,
        grid=(n_cols // MM_TN, s // MM_TM),
        in_specs=[pl.BlockSpec((MM_TM, k), lambda j, i: (i, 0)), _weight_spec(w, layer, 0)],
        out_specs=[pl.BlockSpec((MM_TM, MM_TN), lambda j, i: (i, j)),
                   pl.BlockSpec((d4, MM_TM // d4, MM_TN), lambda j, i: (0, i, j)),
                   pl.BlockSpec((d16, MM_TM // d16, MM_TN), lambda j, i: (0, i, j))],
        out_shape=[jax.ShapeDtypeStruct((s, n_cols), BF16),
                   jax.ShapeDtypeStruct((d4, s // d4, n_cols), BF16),
                   jax.ShapeDtypeStruct((d16, s // d16, n_cols), BF16)],
        scratch_shapes=[pltpu.VMEM((k, MM_TN), BF16),
                        pltpu.VMEM((MM_TN // 128, MM_TM, 128), F32),
                        pltpu.VMEM((MM_TN // 128, d4, MM_TM // d4, 128), F32)],
        compiler_params=_params(("arbitrary", "arbitrary"), 56),
        name="in_proj_dilated",
    )(u, w)


def _matmul(lhs, w, layer, col_block0, n_cols, out_dtype, name):
    s = lhs[0].shape[0]
    k = w.shape[-2]
    assert sum(a.shape[1] for a in lhs) == k
    return pl.pallas_call(
        functools.partial(_mm_kernel, n_lhs=len(lhs)),
        grid=(n_cols // MM_TN, s // MM_TM),
        in_specs=[pl.BlockSpec((MM_TM, a.shape[1]), lambda j, i: (i, 0)) for a in lhs]
        + [_weight_spec(w, layer, col_block0)],
        out_specs=pl.BlockSpec((MM_TM, MM_TN), lambda j, i: (i, j)),
        out_shape=jax.ShapeDtypeStruct((s, n_cols), out_dtype),
        scratch_shapes=[pltpu.VMEM((k, MM_TN), BF16)],
        compiler_params=_params(("arbitrary", "arbitrary"), 56),
        name=name,
    )(*lhs, w)


def _band_bias(slope, dist_scale, max_dist):
    qi = lax.broadcasted_iota(jnp.int32, (BLOCK, 2 * BLOCK), 0)
    kj = lax.broadcasted_iota(jnp.int32, (BLOCK, 2 * BLOCK), 1)
    dist = qi + BLOCK - kj
    band = (dist >= 0) & (dist <= max_dist)
    bias = jnp.where(band, -(slope * dist_scale) * dist.astype(F32), NEG)
    return bias, jnp.where(kj >= BLOCK, bias, NEG)


def _scores(q, kk, scale, bias):
    s = lax.dot_general(q, kk, (((1,), (1,)), ((), ())), preferred_element_type=F32)
    return s * scale + bias


def _attn_a_kernel(slope_ref,
                   q1_ref, k1_ref, v1_ref, kp1_ref, vp1_ref,
                   q4_ref, k4_ref, v4_ref, kp4_ref, vp4_ref,
                   q16_ref, k16_ref, v16_ref, kp16_ref, vp16_ref,
                   o_ref,
                   kb1, vb1, kb4, vb4, kb16, vb16, bias_ref, acc_ref, lse_ref, nat_ref, lsen_ref):
    for hh in range(ATTN_HEADS):
        _attn_a_head(slope_ref[pl.program_id(0) * ATTN_HEADS + hh], pl.program_id(1),
                     slice(hh * HEAD_DIM_A, (hh + 1) * HEAD_DIM_A),
                     q1_ref, k1_ref, v1_ref, kp1_ref, vp1_ref,
                     q4_ref, k4_ref, v4_ref, kp4_ref, vp4_ref,
                     q16_ref, k16_ref, v16_ref, kp16_ref, vp16_ref, o_ref,
                     kb1, vb1, kb4, vb4, kb16, vb16, bias_ref, acc_ref, lse_ref, nat_ref, lsen_ref)


def _attn_a_head(slope, sp, cs,
                 q1_ref, k1_ref, v1_ref, kp1_ref, vp1_ref,
                 q4_ref, k4_ref, v4_ref, kp4_ref, vp4_ref,
                 q16_ref, k16_ref, v16_ref, kp16_ref, vp16_ref, o_ref,
                 kb1, vb1, kb4, vb4, kb16, vb16, bias_ref, acc_ref, lse_ref, nat_ref, lsen_ref):
    scale = HEAD_DIM_A ** -0.5
    d4, d16 = DILATIONS[1], DILATIONS[2]
    n4 = SPAN // d4

    for bi, d in enumerate(DILATIONS):
        b0, b1 = _band_bias(slope, float(d), MAX_DIST_A)
        bias_ref[2 * bi] = b0
        bias_ref[2 * bi + 1] = b1

    kb1[0:BLOCK] = kp1_ref[:, cs]
    kb1[BLOCK:] = k1_ref[:, cs]
    vb1[0:BLOCK] = vp1_ref[:, cs]
    vb1[BLOCK:] = v1_ref[:, cs]
    kb4[:, 0:BLOCK] = kp4_ref[:, :, cs]
    kb4[:, BLOCK:] = k4_ref[:, :, cs]
    vb4[:, 0:BLOCK] = vp4_ref[:, :, cs]
    vb4[:, BLOCK:] = v4_ref[:, :, cs]
    kb16[:, 0:BLOCK] = kp16_ref[:, :, cs]
    kb16[:, BLOCK:] = k16_ref[:, :, cs]
    vb16[:, 0:BLOCK] = vp16_ref[:, :, cs]
    vb16[:, BLOCK:] = v16_ref[:, :, cs]

    def block(q, kk, vv, bias):
        s = _scores(q, kk, scale, bias)
        m = jnp.max(s, axis=-1, keepdims=True)
        p = jnp.exp(s - m)
        l = jnp.sum(p, axis=-1, keepdims=True)
        o = jnp.dot(p.astype(BF16), vv, preferred_element_type=F32) * (1.0 / l)
        return o, jnp.broadcast_to(m + jnp.log(l), o.shape)

    def body1(b, carry):
        r0 = pl.multiple_of(b * BLOCK, BLOCK)
        first = ((sp == 0) & (b == 0)).astype(jnp.int32)
        o, lse = block(q1_ref[pl.ds(r0, BLOCK), cs], kb1[pl.ds(r0, 2 * BLOCK), :],
                       vb1[pl.ds(r0, 2 * BLOCK), :], bias_ref[first])
        acc_ref[0, pl.ds(r0, BLOCK), :] = o
        lse_ref[0, pl.ds(r0, BLOCK), :] = lse
        return carry

    lax.fori_loop(0, SPAN // BLOCK, body1, 0, unroll=ATTN_UNROLL)

    def body4(idx, carry):
        r = idx // (n4 // BLOCK)
        b = idx % (n4 // BLOCK)
        r0 = pl.multiple_of(b * BLOCK, BLOCK)
        first = ((sp == 0) & (b == 0)).astype(jnp.int32)
        o, lse = block(q4_ref[r, pl.ds(r0, BLOCK), cs], kb4[r, pl.ds(r0, 2 * BLOCK), :],
                       vb4[r, pl.ds(r0, 2 * BLOCK), :], bias_ref[2 + first])
        w0 = pl.multiple_of(idx * BLOCK, BLOCK)
        acc_ref[1, pl.ds(w0, BLOCK), :] = o
        lse_ref[1, pl.ds(w0, BLOCK), :] = lse
        return carry

    lax.fori_loop(0, SPAN // BLOCK, body4, 0, unroll=ATTN_UNROLL)

    def body16(r, carry):
        first = (sp == 0).astype(jnp.int32)
        o, lse = block(q16_ref[r, :, cs], kb16[r], vb16[r], bias_ref[4 + first])
        w0 = pl.multiple_of(r * BLOCK, BLOCK)
        acc_ref[2, pl.ds(w0, BLOCK), :] = o
        lse_ref[2, pl.ds(w0, BLOCK), :] = lse
        return carry

    lax.fori_loop(0, d16, body16, 0, unroll=ATTN_UNROLL)

    for r in range(d4):
        nat_ref[0, pl.ds(r, n4, stride=d4), :] = acc_ref[1, r * n4:(r + 1) * n4, :]
        lsen_ref[0, pl.ds(r, n4, stride=d4), :] = lse_ref[1, r * n4:(r + 1) * n4, :]
    for src, dst in ((acc_ref, nat_ref), (lse_ref, lsen_ref)):
        for r in range(d16):
            src[1, pl.ds((r % d4) * n4 + r // d4, BLOCK, stride=d4), :] = (
                src[2, r * BLOCK:(r + 1) * BLOCK, :])
        for r in range(d4):
            dst[1, pl.ds(r, n4, stride=d4), :] = src[1, r * n4:(r + 1) * n4, :]

    chunk = 256

    def merge(c, carry):
        r0 = pl.multiple_of(c * chunk, chunk)
        rows = pl.ds(r0, chunk)
        l1 = lse_ref[0, rows, :]
        l4 = lsen_ref[0, rows, :]
        l16 = lsen_ref[1, rows, :]
        mx = jnp.maximum(jnp.maximum(l1, l4), l16)
        e1 = jnp.exp(l1 - mx)
        e4 = jnp.exp(l4 - mx)
        e16 = jnp.exp(l16 - mx)
        inv = 1.0 / (e1 + e4 + e16)
        o = ((e1 * inv) * acc_ref[0, rows, :] + (e4 * inv) * nat_ref[0, rows, :]
             + (e16 * inv) * nat_ref[1, rows, :])
        o_ref[rows, cs] = o.astype(o_ref.dtype)
        return carry

    lax.fori_loop(0, SPAN // chunk, merge, 0)


def _attn_a(h1, h4, h16, slopes_a):
    s = h1.shape[0]
    d4, d16 = DILATIONS[1], DILATIONS[2]
    n4 = SPAN // d4
    nh = N_HEADS_A // ATTN_HEADS
    hw = ATTN_HEADS * HEAD_DIM_A
    blk1 = SPAN // BLOCK

    def cur1(off):
        return pl.BlockSpec((SPAN, hw), lambda h, sp: (sp, off + h))

    def prev1(off):
        return pl.BlockSpec((BLOCK, hw), lambda h, sp: (jnp.maximum(sp * blk1 - 1, 0), off + h))

    def cur4(off):
        return pl.BlockSpec((d4, n4, hw), lambda h, sp: (0, sp, off + h))

    def prev4(off):
        return pl.BlockSpec((d4, BLOCK, hw),
                            lambda h, sp: (0, jnp.maximum(sp * (n4 // BLOCK) - 1, 0), off + h))

    def cur16(off):
        return pl.BlockSpec((d16, BLOCK, hw), lambda h, sp: (0, sp, off + h))

    def prev16(off):
        return pl.BlockSpec((d16, BLOCK, hw), lambda h, sp: (0, jnp.maximum(sp - 1, 0), off + h))

    in_specs = [pl.BlockSpec(memory_space=pltpu.SMEM)]
    args = [slopes_a]
    for arr, cur, prev in ((h1, cur1, prev1), (h4, cur4, prev4), (h16, cur16, prev16)):
        in_specs += [cur(0), cur(nh), cur(2 * nh), prev(nh), prev(2 * nh)]
        args += [arr] * 5
    return pl.pallas_call(
        _attn_a_kernel,
        grid=(nh, s // SPAN),
        in_specs=in_specs,
        out_specs=pl.BlockSpec((SPAN, hw), lambda h, sp: (sp, h)),
        out_shape=jax.ShapeDtypeStruct((s, WIDTH_A), BF16),
        scratch_shapes=[
            pltpu.VMEM((BLOCK + SPAN, HEAD_DIM_A), BF16), pltpu.VMEM((BLOCK + SPAN, HEAD_DIM_A), BF16),
            pltpu.VMEM((d4, BLOCK + n4, HEAD_DIM_A), BF16), pltpu.VMEM((d4, BLOCK + n4, HEAD_DIM_A), BF16),
            pltpu.VMEM((d16, 2 * BLOCK, HEAD_DIM_A), BF16), pltpu.VMEM((d16, 2 * BLOCK, HEAD_DIM_A), BF16),
            pltpu.VMEM((6, BLOCK, 2 * BLOCK), F32),
            pltpu.VMEM((3, SPAN, HEAD_DIM_A), F32),
            pltpu.VMEM((3, SPAN, HEAD_DIM_A), F32),
            pltpu.VMEM((2, SPAN, HEAD_DIM_A), F32),
            pltpu.VMEM((2, SPAN, HEAD_DIM_A), F32),
        ],
        compiler_params=_params(("arbitrary", "arbitrary"), 48),
        name="attn_dilated",
    )(*args)


def _attn_b_kernel(slope_ref, sink_ref, q_ref, k_ref, v_ref, kp_ref, vp_ref, o_ref, kb, vb, bias_ref):
    hk = pl.program_id(0)
    sp = pl.program_id(1)
    scale = HEAD_DIM_B ** -0.5

    for g in range(GQA_GROUP):
        b0, b1 = _band_bias(slope_ref[hk * GQA_GROUP + g], 1.0, MAX_DIST_B)
        bias_ref[2 * g] = b0
        bias_ref[2 * g + 1] = b1

    src = lax.broadcasted_iota(jnp.int32, (128, 128), 0)
    dst = lax.broadcasted_iota(jnp.int32, (128, 128), 1)
    pick = (src == (hk % 2) * HEAD_DIM_B + dst % HEAD_DIM_B).astype(BF16)

    def dup(ref):
        return jnp.dot(ref[...], pick, preferred_element_type=F32).astype(BF16)

    kb[0:BLOCK] = dup(kp_ref)
    kb[BLOCK:] = dup(k_ref)
    vb[0:BLOCK] = dup(vp_ref)
    vb[BLOCK:] = dup(v_ref)

    lane = lax.broadcasted_iota(jnp.int32, (BLOCK, 2 * HEAD_DIM_B), 1)
    low = lane < HEAD_DIM_B

    def body(b, carry):
        r0 = pl.multiple_of(b * BLOCK, BLOCK)
        first = ((sp == 0) & (b == 0)).astype(jnp.int32)
        kk = kb[pl.ds(r0, 2 * BLOCK), :]
        vv = vb[pl.ds(r0, 2 * BLOCK), :]
        for pr in range(GQA_GROUP // 2):
            q2 = q_ref[pl.ds(r0, BLOCK), pr * 128:(pr + 1) * 128]
            outs = []
            for half in range(2):
                g = 2 * pr + half
                keep = low if half == 0 else jnp.logical_not(low)
                qq = jnp.where(keep, q2, jnp.zeros_like(q2))
                s = _scores(qq, kk, scale, bias_ref[2 * g + first])
                sink = sink_ref[hk * GQA_GROUP + g]
                m = jnp.maximum(jnp.max(s, axis=-1, keepdims=True), sink)
                p = jnp.exp(s - m)
                l = jnp.sum(p, axis=-1, keepdims=True) + jnp.exp(sink - m)
                outs.append(jnp.dot(p.astype(BF16), vv, preferred_element_type=F32) * (1.0 / l))
            o_ref[pl.ds(r0, BLOCK), pr * 128:(pr + 1) * 128] = jnp.where(low, outs[0], outs[1]).astype(o_ref.dtype)
        return carry

    lax.fori_loop(0, SPAN // BLOCK, body, 0, unroll=4)


def _attn_b(hb, slopes_b, sinks):
    s = hb.shape[0]
    gw = GQA_GROUP * HEAD_DIM_B
    blk = SPAN // BLOCK
    k_off = WIDTH_B // 128
    v_off = k_off + KV_WIDTH_B // 128

    def cur(off):
        return pl.BlockSpec((SPAN, 128), lambda h, sp: (sp, off + h // 2))

    def prev(off):
        return pl.BlockSpec((BLOCK, 128), lambda h, sp: (jnp.maximum(sp * blk - 1, 0), off + h // 2))

    smem = pl.BlockSpec(memory_space=pltpu.SMEM)
    return pl.pallas_call(
        _attn_b_kernel,
        grid=(N_KV_HEADS_B, s // SPAN),
        in_specs=[smem, smem, pl.BlockSpec((SPAN, gw), lambda h, sp: (sp, h)),
                  cur(k_off), cur(v_off), prev(k_off), prev(v_off)],
        out_specs=pl.BlockSpec((SPAN, gw), lambda h, sp: (sp, h)),
        out_shape=jax.ShapeDtypeStruct((s, WIDTH_B), BF16),
        scratch_shapes=[pltpu.VMEM((BLOCK + SPAN, 128), BF16), pltpu.VMEM((BLOCK + SPAN, 128), BF16),
                        pltpu.VMEM((2 * GQA_GROUP, BLOCK, 2 * BLOCK), F32)],
        compiler_params=_params(("arbitrary", "arbitrary"), 48),
        name="attn_swa",
    )(slopes_b, sinks, hb, hb, hb, hb, hb)


def _mlp_kernel(u_ref, wg_ref, wv_ref, cwg_ref, cwv_ref, cbg_ref, cbv_ref, wd_ref, o_ref,
                hb0, hb1, carry, *, nj):
    t = pl.program_id(0)
    tm = u_ref.shape[0]
    tn = wg_ref.shape[1]
    td = jnp.maximum(t - 1, 0)
    i_d = td // nj
    j_d = td % nj

    @pl.when(t == 0)
    def _():
        hb1[...] = jnp.zeros(hb1.shape, F32)

    def step(hw, hr):
        hr[0:8, :] = jnp.where(i_d == 0, 0.0, carry[j_d])
        carry[j_d] = hr[tm:tm + 8, :]

        def conv(c0, c1, cw_ref, cb_ref):
            acc = cw_ref[0:1, :] * hr[8 - 2:8 - 2 + tm, c0:c1]
            acc = acc + cw_ref[1:2, :] * hr[8 - 1:8 - 1 + tm, c0:c1]
            acc = acc + cw_ref[2:3, :] * hr[8:8 + tm, c0:c1]
            return acc + cb_ref[...]

        gate = conv(0, tn, cwg_ref, cbg_ref)
        val = conv(tn, 2 * tn, cwv_ref, cbv_ref)
        act = (gate * jax.nn.sigmoid(gate) * val).astype(BF16)
        contrib = jnp.dot(act, wd_ref[...], preferred_element_type=F32)
        o_ref[...] = jnp.where(j_d == 0, contrib, o_ref[...] + contrib)

        u = u_ref[...]
        hw[8:, 0:tn] = jnp.dot(u, wg_ref[...], preferred_element_type=F32)
        hw[8:, tn:] = jnp.dot(u, wv_ref[...], preferred_element_type=F32)

    @pl.when(t % 2 == 0)
    def _():
        step(hb0, hb1)

    @pl.when(t % 2 == 1)
    def _():
        step(hb1, hb0)


def _conv_ffn(u, w_up, conv_w, conv_b, w_down):
    s, d = u.shape
    nj = FF_PAD // MLP_TN
    n_chunks = (s // MLP_TM) * nj
    last = n_chunks - 1

    def chunk(lag):
        def ij(t):
            c = jnp.clip(t - lag, 0, last)
            return c // nj, c % nj
        return ij

    up, dn = chunk(0), chunk(1)
    return pl.pallas_call(
        functools.partial(_mlp_kernel, nj=nj),
        grid=(n_chunks + 1,),
        in_specs=[pl.BlockSpec((MLP_TM, d), lambda t: (up(t)[0], 0)),
                  pl.BlockSpec((d, MLP_TN), lambda t: (0, up(t)[1])),
                  pl.BlockSpec((d, MLP_TN), lambda t: (0, nj + up(t)[1])),
                  pl.BlockSpec((CONV_WIDTH, MLP_TN), lambda t: (0, dn(t)[1])),
                  pl.BlockSpec((CONV_WIDTH, MLP_TN), lambda t: (0, nj + dn(t)[1])),
                  pl.BlockSpec((1, MLP_TN), lambda t: (0, dn(t)[1])),
                  pl.BlockSpec((1, MLP_TN), lambda t: (0, nj + dn(t)[1])),
                  pl.BlockSpec((MLP_TN, d), lambda t: (dn(t)[1], 0))],
        out_specs=pl.BlockSpec((MLP_TM, d), lambda t: (dn(t)[0], 0)),
        out_shape=jax.ShapeDtypeStruct((s, d), F32),
        scratch_shapes=[pltpu.VMEM((8 + MLP_TM, 2 * MLP_TN), F32),
                        pltpu.VMEM((8 + MLP_TM, 2 * MLP_TN), F32),
                        pltpu.VMEM((nj, 8, 2 * MLP_TN), F32)],
        compiler_params=_params(("arbitrary",), 60),
        name="conv_ffn",
    )(u, w_up, w_up, conv_w, conv_w, conv_b, conv_b, w_down)


def _pad_ff(a, axis):
    pad = [(0, 0)] * a.ndim
    pad[axis] = (0, FF_PAD - D_FF)
    g, v = jnp.split(a, 2, axis=axis)
    return jnp.concatenate([jnp.pad(g, pad), jnp.pad(v, pad)], axis=axis)


def _cast_pad_kernel(w_ref, o_ref, *, axis, n_valid):
    c = pl.program_id(axis)

    @pl.when(c < n_valid)
    def _():
        o_ref[...] = w_ref[...].astype(o_ref.dtype)

    @pl.when(c >= n_valid)
    def _():
        o_ref[...] = jnp.zeros(o_ref.shape, o_ref.dtype)


def _cast_w_up(w_up, layer):
    d = w_up.shape[1]
    nv, npad = D_FF // CAST_BLK, FF_PAD // CAST_BLK
    return pl.pallas_call(
        functools.partial(_cast_pad_kernel, axis=1, n_valid=nv),
        grid=(2, npad),
        in_specs=[pl.BlockSpec((None, d, CAST_BLK),
                               lambda h, c: (layer, 0, h * nv + jnp.minimum(c, nv - 1)))],
        out_specs=pl.BlockSpec((d, CAST_BLK), lambda h, c: (0, h * npad + c)),
        out_shape=jax.ShapeDtypeStruct((d, 2 * FF_PAD), BF16),
        compiler_params=_params(("arbitrary", "arbitrary"), 32),
        name="cast_w_up",
    )(w_up)


def _cast_w_down(w_down, layer):
    d = w_down.shape[2]
    nv, npad = D_FF // CAST_BLK, FF_PAD // CAST_BLK
    return pl.pallas_call(
        functools.partial(_cast_pad_kernel, axis=0, n_valid=nv),
        grid=(npad,),
        in_specs=[pl.BlockSpec((None, CAST_BLK, d), lambda c: (layer, jnp.minimum(c, nv - 1), 0))],
        out_specs=pl.BlockSpec((CAST_BLK, d), lambda c: (c, 0)),
        out_shape=jax.ShapeDtypeStruct((FF_PAD, d), BF16),
        compiler_params=_params(("arbitrary",), 32),
        name="cast_w_down",
    )(w_down)


def _alibi_slopes():
    i = jnp.arange(1, N_ALIBI_HEADS + 1, dtype=F32)
    return jnp.exp2(-8.0 * i / N_ALIBI_HEADS)


def kernel(x, c, w_mod, b_mod, w_in, sinks, w_out, ln1_g, ln1_b, w_up, conv_w, conv_b, w_down, ln2_g, ln2_b):
    b, s, d = x.shape
    assert (b, s, d) == (1, SEQ, D_MODEL)
    slopes = _alibi_slopes()
    slopes_b, slopes_a = slopes[:N_Q_HEADS_B], slopes[N_Q_HEADS_B:]
    mod = _mod_all(c, w_mod, b_mod)
    x = x.reshape(s, d)
    u = None
    for l in range(DEPTH):
        shift_a, scale_a, gate_a, shift_m, scale_m, gate_m = (
            mod[l, :, n * d:(n + 1) * d] for n in range(N_MOD))
        w_up_l = _cast_w_up(w_up, l)
        w_down_l = _cast_w_down(w_down, l)
        conv_w_l = _pad_ff(conv_w[l], 1)
        conv_b_l = _pad_ff(conv_b[l].reshape(1, 2 * D_FF), 1)

        if u is None:
            u = _modulate(x, scale_a, shift_a)
        h1, h4, h16 = _in_proj_dilated(u, w_in, l, QKV_A)
        hb = _matmul([u], w_in, l, QKV_A // MM_TN, WIDTH_B + 2 * KV_WIDTH_B, BF16, "in_proj_b")
        mixed_a = _attn_a(h1, h4, h16, slopes_a)
        mixed_b = _attn_b(hb, slopes_b, sinks[l])
        att = _matmul([mixed_a, mixed_b], w_out, l, 0, d, F32, "out_proj")
        x, u = _residual_ln(x, att, gate_a, ln1_g[l].reshape(1, d), ln1_b[l].reshape(1, d),
                            (scale_m, shift_m))
        y = _conv_ffn(u, w_up_l, conv_w_l, conv_b_l, w_down_l)
        nxt = None
        if l + 1 < DEPTH:
            nxt = (mod[l + 1, :, d:2 * d], mod[l + 1, :, 0:d])
        x, u = _residual_ln(x, y, gate_m, ln2_g[l].reshape(1, d), ln2_b[l].reshape(1, d), nxt)
    return x.reshape(b, s, d)
```

```python
import functools
import math

import jax
import jax.numpy as jnp
from jax import lax
from jax.experimental import pallas as pl
from jax.experimental.pallas import tpu as pltpu

F32 = jnp.float32
BF16 = jnp.bfloat16

D_MODEL = 4096
SEQ = 8192
DEPTH = 2
HEAD_DIM_A = 128
N_HEADS_A = 16
DILATIONS = (1, 4, 16)
MAX_DIST_A = 128
HEAD_DIM_B = 64
N_Q_HEADS_B = 32
N_KV_HEADS_B = 4
GQA_GROUP = 8
MAX_DIST_B = 127
WIDTH_A = N_HEADS_A * HEAD_DIM_A
WIDTH_B = N_Q_HEADS_B * HEAD_DIM_B
KV_WIDTH_B = N_KV_HEADS_B * HEAD_DIM_B
QKV_A = 3 * WIDTH_A
N_ALIBI_HEADS = N_HEADS_A + N_Q_HEADS_B
D_FF = 11008
CONV_WIDTH = 3
N_MOD = 6
DEEPNORM_ALPHA = (2 * DEPTH) ** 0.25
LN_EPS = 1e-5
BLOCK = 128
NEG = -1e30

SPAN = 2048
MM_TM = 1024
MM_TN = 512
MLP_TM = 512
MLP_TN = 512
FF_PAD = MLP_TN * math.ceil(D_FF / MLP_TN)
LN_TM = 256
CAST_BLK = 256
ATTN_HEADS = 1
ATTN_UNROLL = 16
MOD_TN = 512
MIB = 1024 * 1024


def _params(sem, vmem_mib):
    return pltpu.CompilerParams(dimension_semantics=sem, vmem_limit_bytes=vmem_mib * MIB)


def _mod_kernel(c_ref, w_ref, b_ref, o_ref, cb_ref):
    @pl.when((pl.program_id(0) == 0) & (pl.program_id(1) == 0))
    def _():
        cc = c_ref[...]
        cb_ref[...] = jnp.broadcast_to(cc * jax.nn.sigmoid(cc), cb_ref.shape)

    k_dim = w_ref.shape[1]
    n_grp = w_ref.shape[2] // 128

    def body(k, accs):
        r = pl.multiple_of(k * 8, 8)
        cb = cb_ref[pl.ds(r, 8), :]
        return tuple(a + w_ref[0, pl.ds(r, 8), g * 128:(g + 1) * 128] * cb
                     for g, a in enumerate(accs))

    accs = lax.fori_loop(0, k_dim // 8, body,
                         tuple(jnp.zeros((8, 128), F32) for _ in range(n_grp)), unroll=8)
    for g, a in enumerate(accs):
        o_ref[0, :, g * 128:(g + 1) * 128] = (jnp.sum(a, axis=0, keepdims=True)
                                              + b_ref[0, :, g * 128:(g + 1) * 128])


def _mod_all(c, w_mod, b_mod):
    depth, d, n = w_mod.shape
    return pl.pallas_call(
        _mod_kernel,
        grid=(depth, n // MOD_TN),
        in_specs=[pl.BlockSpec((d, 1), lambda l, j: (0, 0)),
                  pl.BlockSpec((1, d, MOD_TN), lambda l, j: (l, 0, j)),
                  pl.BlockSpec((1, 1, MOD_TN), lambda l, j: (l, 0, j))],
        out_specs=pl.BlockSpec((1, 1, MOD_TN), lambda l, j: (l, 0, j)),
        out_shape=jax.ShapeDtypeStruct((depth, 1, n), F32),
        scratch_shapes=[pltpu.VMEM((d, 128), F32)],
        compiler_params=_params(("arbitrary", "arbitrary"), 40),
        name="mod_matvec",
    )(c.reshape(d, 1), w_mod, b_mod.reshape(depth, 1, n))


def _modulate_kernel(x_ref, sc_ref, sh_ref, u_ref):
    u_ref[...] = (x_ref[...] * (1.0 + sc_ref[...]) + sh_ref[...]).astype(u_ref.dtype)


def _modulate(x, scale, shift):
    s, d = x.shape
    row = pl.BlockSpec((1, d), lambda i: (0, 0))
    return pl.pallas_call(
        _modulate_kernel,
        grid=(s // LN_TM,),
        in_specs=[pl.BlockSpec((LN_TM, d), lambda i: (i, 0)), row, row],
        out_specs=pl.BlockSpec((LN_TM, d), lambda i: (i, 0)),
        out_shape=jax.ShapeDtypeStruct((s, d), BF16),
        compiler_params=_params(("arbitrary",), 32),
        name="modulate",
    )(x, scale, shift)


def _ln_kernel(x_ref, y_ref, gate_ref, g_ref, b_ref, *rest, with_next):
    z = DEEPNORM_ALPHA * x_ref[...] + (1.0 + gate_ref[...]) * y_ref[...]
    mu = jnp.mean(z, axis=-1, keepdims=True)
    zc = z - mu
    var = jnp.mean(zc * zc, axis=-1, keepdims=True)
    xn = zc * lax.rsqrt(var + LN_EPS) * g_ref[...] + b_ref[...]
    if with_next:
        sc_ref, sh_ref, xo_ref, uo_ref = rest
        xo_ref[...] = xn
        uo_ref[...] = (xn * (1.0 + sc_ref[...]) + sh_ref[...]).astype(uo_ref.dtype)
    else:
        (xo_ref,) = rest
        xo_ref[...] = xn


def _residual_ln(x, y, gate, g, b, nxt=None):
    s, d = x.shape
    row = pl.BlockSpec((1, d), lambda i: (0, 0))
    tile = pl.BlockSpec((LN_TM, d), lambda i: (i, 0))
    ins = [x, y, gate, g, b]
    in_specs = [tile, tile, row, row, row]
    out_shape = [jax.ShapeDtypeStruct((s, d), F32)]
    out_specs = [tile]
    if nxt is not None:
        ins += list(nxt)
        in_specs += [row, row]
        out_shape.append(jax.ShapeDtypeStruct((s, d), BF16))
        out_specs.append(tile)
    res = pl.pallas_call(
        functools.partial(_ln_kernel, with_next=nxt is not None),
        grid=(s // LN_TM,),
        in_specs=in_specs, out_specs=out_specs, out_shape=out_shape,
        compiler_params=_params(("arbitrary",), 48),
        name="residual_ln",
    )(*ins)
    return res if nxt is not None else (res[0], None)


def _mm_kernel(*refs, n_lhs):
    lhs = refs[:n_lhs]
    w_ref, o_ref, wb_ref = refs[n_lhs:]

    @pl.when(pl.program_id(1) == 0)
    def _():
        wb_ref[...] = w_ref[...].astype(wb_ref.dtype)

    acc = None
    k0 = 0
    for a in lhs:
        kk = a.shape[1]
        part = jnp.dot(a[...], wb_ref[k0:k0 + kk, :], preferred_element_type=F32)
        acc = part if acc is None else acc + part
        k0 += kk
    o_ref[...] = acc.astype(o_ref.dtype)


def _mm_dilated_kernel(u_ref, w_ref, o1_ref, o4_ref, o16_ref, wb_ref, acc_ref, s4_ref):
    @pl.when(pl.program_id(1) == 0)
    def _():
        wb_ref[...] = w_ref[...].astype(wb_ref.dtype)

    n_cb, tm, _ = acc_ref.shape
    d4 = DILATIONS[1]
    assert DILATIONS[2] == d4 * d4
    n4 = tm // d4
    acc = jnp.dot(u_ref[...], wb_ref[...], preferred_element_type=F32)
    o1_ref[...] = acc.astype(o1_ref.dtype)
    for cb in range(n_cb):
        cols = slice(cb * 128, (cb + 1) * 128)
        acc_ref[cb] = acc[:, cols]
        for r in range(d4):
            rows4 = acc_ref[cb, pl.ds(r, n4, stride=d4), :]
            o4_ref[r, :, cols] = rows4.astype(o4_ref.dtype)
            s4_ref[cb, r] = rows4
        for r in range(d4):
            for rr in range(d4):
                o16_ref[r + d4 * rr, :, cols] = (
                    s4_ref[cb, r, pl.ds(rr, n4 // d4, stride=d4), :].astype(o16_ref.dtype))


def _weight_spec(w, layer, col_block0):
    k = w.shape[-2]
    if w.ndim == 3:
        return pl.BlockSpec((None, k, MM_TN), lambda j, i: (layer, 0, col_block0 + j))
    return pl.BlockSpec((k, MM_TN), lambda j, i: (0, col_block0 + j))


def _in_proj_dilated(u, w, layer, n_cols):
    s, k = u.shape
    d4, d16 = DILATIONS[1], DILATIONS[2]
    return pl.pallas_call(
        _mm_dilated_kernel,
        grid=(n_cols // MM_TN, s // MM_TM),
        in_specs=[pl.BlockSpec((MM_TM, k), lambda j, i: (i, 0)), _weight_spec(w, layer, 0)],
        out_specs=[pl.BlockSpec((MM_TM, MM_TN), lambda j, i: (i, j)),
                   pl.BlockSpec((d4, MM_TM // d4, MM_TN), lambda j, i: (0, i, j)),
                   pl.BlockSpec((d16, MM_TM // d16, MM_TN), lambda j, i: (0, i, j))],
        out_shape=[jax.ShapeDtypeStruct((s, n_cols), BF16),
                   jax.ShapeDtypeStruct((d4, s // d4, n_cols), BF16),
                   jax.ShapeDtypeStruct((d16, s // d16, n_cols), BF16)],
        scratch_shapes=[pltpu.VMEM((k, MM_TN), BF16),
                        pltpu.VMEM((MM_TN // 128, MM_TM, 128), F32),
                        pltpu.VMEM((MM_TN // 128, d4, MM_TM // d4, 128), F32)],
        compiler_params=_params(("arbitrary", "arbitrary"), 56),
        name="in_proj_dilated",
    )(u, w)


def _matmul(lhs, w, layer, col_block0, n_cols, out_dtype, name):
    s = lhs[0].shape[0]
    k = w.shape[-2]
    assert sum(a.shape[1] for a in lhs) == k
    return pl.pallas_call(
        functools.partial(_mm_kernel, n_lhs=len(lhs)),
        grid=(n_cols // MM_TN, s // MM_TM),
        in_specs=[pl.BlockSpec((MM_TM, a.shape[1]), lambda j, i: (i, 0)) for a in lhs]
        + [_weight_spec(w, layer, col_block0)],
        out_specs=pl.BlockSpec((MM_TM, MM_TN), lambda j, i: (i, j)),
        out_shape=jax.ShapeDtypeStruct((s, n_cols), out_dtype),
        scratch_shapes=[pltpu.VMEM((k, MM_TN), BF16)],
        compiler_params=_params(("arbitrary", "arbitrary"), 56),
        name=name,
    )(*lhs, w)


def _band_bias(slope, dist_scale, max_dist):
    qi = lax.broadcasted_iota(jnp.int32, (BLOCK, 2 * BLOCK), 0)
    kj = lax.broadcasted_iota(jnp.int32, (BLOCK, 2 * BLOCK), 1)
    dist = qi + BLOCK - kj
    band = (dist >= 0) & (dist <= max_dist)
    bias = jnp.where(band, -(slope * dist_scale) * dist.astype(F32), NEG)
    return bias, jnp.where(kj >= BLOCK, bias, NEG)


def _scores(q, kk, scale, bias):
    s = lax.dot_general(q, kk, (((1,), (1,)), ((), ())), preferred_element_type=F32)
    return s * scale + bias


def _attn_a_kernel(slope_ref,
                   q1_ref, k1_ref, v1_ref, kp1_ref, vp1_ref,
                   q4_ref, k4_ref, v4_ref, kp4_ref, vp4_ref,
                   q16_ref, k16_ref, v16_ref, kp16_ref, vp16_ref,
                   o_ref,
                   kb1, vb1, kb4, vb4, kb16, vb16, bias_ref, acc_ref, lse_ref, nat_ref, lsen_ref):
    for hh in range(ATTN_HEADS):
        _attn_a_head(slope_ref[pl.program_id(0) * ATTN_HEADS + hh], pl.program_id(1),
                     slice(hh * HEAD_DIM_A, (hh + 1) * HEAD_DIM_A),
                     q1_ref, k1_ref, v1_ref, kp1_ref, vp1_ref,
                     q4_ref, k4_ref, v4_ref, kp4_ref, vp4_ref,
                     q16_ref, k16_ref, v16_ref, kp16_ref, vp16_ref, o_ref,
                     kb1, vb1, kb4, vb4, kb16, vb16, bias_ref, acc_ref, lse_ref, nat_ref, lsen_ref)


def _attn_a_head(slope, sp, cs,
                 q1_ref, k1_ref, v1_ref, kp1_ref, vp1_ref,
                 q4_ref, k4_ref, v4_ref, kp4_ref, vp4_ref,
                 q16_ref, k16_ref, v16_ref, kp16_ref, vp16_ref, o_ref,
                 kb1, vb1, kb4, vb4, kb16, vb16, bias_ref, acc_ref, lse_ref, nat_ref, lsen_ref):
    scale = HEAD_DIM_A ** -0.5
    d4, d16 = DILATIONS[1], DILATIONS[2]
    n4 = SPAN // d4

    for bi, d in enumerate(DILATIONS):
        b0, b1 = _band_bias(slope, float(d), MAX_DIST_A)
        bias_ref[2 * bi] = b0
        bias_ref[2 * bi + 1] = b1

    kb1[0:BLOCK] = kp1_ref[:, cs]
    kb1[BLOCK:] = k1_ref[:, cs]
    vb1[0:BLOCK] = vp1_ref[:, cs]
    vb1[BLOCK:] = v1_ref[:, cs]
    kb4[:, 0:BLOCK] = kp4_ref[:, :, cs]
    kb4[:, BLOCK:] = k4_ref[:, :, cs]
    vb4[:, 0:BLOCK] = vp4_ref[:, :, cs]
    vb4[:, BLOCK:] = v4_ref[:, :, cs]
    kb16[:, 0:BLOCK] = kp16_ref[:, :, cs]
    kb16[:, BLOCK:] = k16_ref[:, :, cs]
    vb16[:, 0:BLOCK] = vp16_ref[:, :, cs]
    vb16[:, BLOCK:] = v16_ref[:, :, cs]

    def block(q, kk, vv, bias):
        s = _scores(q, kk, scale, bias)
        m = jnp.max(s, axis=-1, keepdims=True)
        p = jnp.exp(s - m)
        l = jnp.sum(p, axis=-1, keepdims=True)
        o = jnp.dot(p.astype(BF16), vv, preferred_element_type=F32) * (1.0 / l)
        return o, jnp.broadcast_to(m + jnp.log(l), o.shape)

    def body1(b, carry):
        r0 = pl.multiple_of(b * BLOCK, BLOCK)
        first = ((sp == 0) & (b == 0)).astype(jnp.int32)
        o, lse = block(q1_ref[pl.ds(r0, BLOCK), cs], kb1[pl.ds(r0, 2 * BLOCK), :],
                       vb1[pl.ds(r0, 2 * BLOCK), :], bias_ref[first])
        acc_ref[0, pl.ds(r0, BLOCK), :] = o
        lse_ref[0, pl.ds(r0, BLOCK), :] = lse
        return carry

    lax.fori_loop(0, SPAN // BLOCK, body1, 0, unroll=ATTN_UNROLL)

    def body4(idx, carry):
        r = idx // (n4 // BLOCK)
        b = idx % (n4 // BLOCK)
        r0 = pl.multiple_of(b * BLOCK, BLOCK)
        first = ((sp == 0) & (b == 0)).astype(jnp.int32)
        o, lse = block(q4_ref[r, pl.ds(r0, BLOCK), cs], kb4[r, pl.ds(r0, 2 * BLOCK), :],
                       vb4[r, pl.ds(r0, 2 * BLOCK), :], bias_ref[2 + first])
        w0 = pl.multiple_of(idx * BLOCK, BLOCK)
        acc_ref[1, pl.ds(w0, BLOCK), :] = o
        lse_ref[1, pl.ds(w0, BLOCK), :] = lse
        return carry

    lax.fori_loop(0, SPAN // BLOCK, body4, 0, unroll=ATTN_UNROLL)

    def body16(r, carry):
        first = (sp == 0).astype(jnp.int32)
        o, lse = block(q16_ref[r, :, cs], kb16[r], vb16[r], bias_ref[4 + first])
        w0 = pl.multiple_of(r * BLOCK, BLOCK)
        acc_ref[2, pl.ds(w0, BLOCK), :] = o
        lse_ref[2, pl.ds(w0, BLOCK), :] = lse
        return carry

    lax.fori_loop(0, d16, body16, 0, unroll=ATTN_UNROLL)

    for r in range(d4):
        nat_ref[0, pl.ds(r, n4, stride=d4), :] = acc_ref[1, r * n4:(r + 1) * n4, :]
        lsen_ref[0, pl.ds(r, n4, stride=d4), :] = lse_ref[1, r * n4:(r + 1) * n4, :]
    for src, dst in ((acc_ref, nat_ref), (lse_ref, lsen_ref)):
        for r in range(d16):
            src[1, pl.ds((r % d4) * n4 + r // d4, BLOCK, stride=d4), :] = (
                src[2, r * BLOCK:(r + 1) * BLOCK, :])
        for r in range(d4):
            dst[1, pl.ds(r, n4, stride=d4), :] = src[1, r * n4:(r + 1) * n4, :]

    chunk = 256

    def merge(c, carry):
        r0 = pl.multiple_of(c * chunk, chunk)
        rows = pl.ds(r0, chunk)
        l1 = lse_ref[0, rows, :]
        l4 = lsen_ref[0, rows, :]
        l16 = lsen_ref[1, rows, :]
        mx = jnp.maximum(jnp.maximum(l1, l4), l16)
        e1 = jnp.exp(l1 - mx)
        e4 = jnp.exp(l4 - mx)
        e16 = jnp.exp(l16 - mx)
        inv = 1.0 / (e1 + e4 + e16)
        o = (e1 * acc_ref[0, rows, :] + e4 * nat_ref[0, rows, :] + e16 * nat_ref[1, rows, :]) * inv
        o_ref[rows, cs] = o.astype(o_ref.dtype)
        return carry

    lax.fori_loop(0, SPAN // chunk, merge, 0)


def _attn_a(h1, h4, h16, slopes_a):
    s = h1.shape[0]
    d4, d16 = DILATIONS[1], DILATIONS[2]
    n4 = SPAN // d4
    nh = N_HEADS_A // ATTN_HEADS
    hw = ATTN_HEADS * HEAD_DIM_A
    blk1 = SPAN // BLOCK

    def cur1(off):
        return pl.BlockSpec((SPAN, hw), lambda h, sp: (sp, off + h))

    def prev1(off):
        return pl.BlockSpec((BLOCK, hw), lambda h, sp: (jnp.maximum(sp * blk1 - 1, 0), off + h))

    def cur4(off):
        return pl.BlockSpec((d4, n4, hw), lambda h, sp: (0, sp, off + h))

    def prev4(off):
        return pl.BlockSpec((d4, BLOCK, hw),
                            lambda h, sp: (0, jnp.maximum(sp * (n4 // BLOCK) - 1, 0), off + h))

    def cur16(off):
        return pl.BlockSpec((d16, BLOCK, hw), lambda h, sp: (0, sp, off + h))

    def prev16(off):
        return pl.BlockSpec((d16, BLOCK, hw), lambda h, sp: (0, jnp.maximum(sp - 1, 0), off + h))

    in_specs = [pl.BlockSpec(memory_space=pltpu.SMEM)]
    args = [slopes_a]
    for arr, cur, prev in ((h1, cur1, prev1), (h4, cur4, prev4), (h16, cur16, prev16)):
        in_specs += [cur(0), cur(nh), cur(2 * nh), prev(nh), prev(2 * nh)]
        args += [arr] * 5
    return pl.pallas_call(
        _attn_a_kernel,
        grid=(nh, s // SPAN),
        in_specs=in_specs,
        out_specs=pl.BlockSpec((SPAN, hw), lambda h, sp: (sp, h)),
        out_shape=jax.ShapeDtypeStruct((s, WIDTH_A), BF16),
        scratch_shapes=[
            pltpu.VMEM((BLOCK + SPAN, HEAD_DIM_A), BF16), pltpu.VMEM((BLOCK + SPAN, HEAD_DIM_A), BF16),
            pltpu.VMEM((d4, BLOCK + n4, HEAD_DIM_A), BF16), pltpu.VMEM((d4, BLOCK + n4, HEAD_DIM_A), BF16),
            pltpu.VMEM((d16, 2 * BLOCK, HEAD_DIM_A), BF16), pltpu.VMEM((d16, 2 * BLOCK, HEAD_DIM_A), BF16),
            pltpu.VMEM((6, BLOCK, 2 * BLOCK), F32),
            pltpu.VMEM((3, SPAN, HEAD_DIM_A), F32),
            pltpu.VMEM((3, SPAN, HEAD_DIM_A), F32),
            pltpu.VMEM((2, SPAN, HEAD_DIM_A), F32),
            pltpu.VMEM((2, SPAN, HEAD_DIM_A), F32),
        ],
        compiler_params=_params(("arbitrary", "arbitrary"), 48),
        name="attn_dilated",
    )(*args)


def _attn_b_kernel(slope_ref, sink_ref, q_ref, k_ref, v_ref, kp_ref, vp_ref, o_ref, kb, vb, bias_ref):
    hk = pl.program_id(0)
    sp = pl.program_id(1)
    scale = HEAD_DIM_B ** -0.5
    assert scale == 2.0 ** -3

    for g in range(GQA_GROUP):
        b0, b1 = _band_bias(slope_ref[hk * GQA_GROUP + g], 1.0, MAX_DIST_B)
        bias_ref[2 * g] = b0
        bias_ref[2 * g + 1] = b1

    src = lax.broadcasted_iota(jnp.int32, (128, 128), 0)
    dst = lax.broadcasted_iota(jnp.int32, (128, 128), 1)
    pick = (src == (hk % 2) * HEAD_DIM_B + dst % HEAD_DIM_B).astype(BF16)

    def dup(ref):
        return jnp.dot(ref[...], pick, preferred_element_type=F32).astype(BF16)

    kb[0:BLOCK] = dup(kp_ref)
    kb[BLOCK:] = dup(k_ref)
    vb[0:BLOCK] = dup(vp_ref)
    vb[BLOCK:] = dup(v_ref)

    lane = lax.broadcasted_iota(jnp.int32, (BLOCK, 2 * HEAD_DIM_B), 1)
    low = lane < HEAD_DIM_B

    def body(b, carry):
        r0 = pl.multiple_of(b * BLOCK, BLOCK)
        first = ((sp == 0) & (b == 0)).astype(jnp.int32)
        kk = kb[pl.ds(r0, 2 * BLOCK), :]
        vv = vb[pl.ds(r0, 2 * BLOCK), :]
        for pr in range(GQA_GROUP // 2):
            q2 = q_ref[pl.ds(r0, BLOCK), pr * 128:(pr + 1) * 128] * scale
            outs = []
            for half in range(2):
                g = 2 * pr + half
                keep = low if half == 0 else jnp.logical_not(low)
                qq = jnp.where(keep, q2, jnp.zeros_like(q2))
                s = lax.dot_general(qq, kk, (((1,), (1,)), ((), ())),
                                    preferred_element_type=F32) + bias_ref[2 * g + first]
                sink = sink_ref[hk * GQA_GROUP + g]
                m = jnp.maximum(jnp.max(s, axis=-1, keepdims=True), sink)
                p = jnp.exp(s - m)
                l = jnp.sum(p, axis=-1, keepdims=True) + jnp.exp(sink - m)
                outs.append(jnp.dot(p.astype(BF16), vv, preferred_element_type=F32) * (1.0 / l))
            o_ref[pl.ds(r0, BLOCK), pr * 128:(pr + 1) * 128] = jnp.where(low, outs[0], outs[1]).astype(o_ref.dtype)
        return carry

    lax.fori_loop(0, SPAN // BLOCK, body, 0, unroll=4)


def _attn_b(hb, slopes_b, sinks):
    s = hb.shape[0]
    gw = GQA_GROUP * HEAD_DIM_B
    blk = SPAN // BLOCK
    k_off = WIDTH_B // 128
    v_off = k_off + KV_WIDTH_B // 128

    def cur(off):
        return pl.BlockSpec((SPAN, 128), lambda h, sp: (sp, off + h // 2))

    def prev(off):
        return pl.BlockSpec((BLOCK, 128), lambda h, sp: (jnp.maximum(sp * blk - 1, 0), off + h // 2))

    smem = pl.BlockSpec(memory_space=pltpu.SMEM)
    return pl.pallas_call(
        _attn_b_kernel,
        grid=(N_KV_HEADS_B, s // SPAN),
        in_specs=[smem, smem, pl.BlockSpec((SPAN, gw), lambda h, sp: (sp, h)),
                  cur(k_off), cur(v_off), prev(k_off), prev(v_off)],
        out_specs=pl.BlockSpec((SPAN, gw), lambda h, sp: (sp, h)),
        out_shape=jax.ShapeDtypeStruct((s, WIDTH_B), BF16),
        scratch_shapes=[pltpu.VMEM((BLOCK + SPAN, 128), BF16), pltpu.VMEM((BLOCK + SPAN, 128), BF16),
                        pltpu.VMEM((2 * GQA_GROUP, BLOCK, 2 * BLOCK), F32)],
        compiler_params=_params(("arbitrary", "arbitrary"), 48),
        name="attn_swa",
    )(slopes_b, sinks, hb, hb, hb, hb, hb)


def _mlp_kernel(u_ref, wg_ref, wv_ref, cwg_ref, cwv_ref, cbg_ref, cbv_ref, wd_ref, o_ref,
                hb0, hb1, carry, *, nj):
    t = pl.program_id(0)
    tm = u_ref.shape[0]
    tn = wg_ref.shape[1]
    td = jnp.maximum(t - 1, 0)
    i_d = td // nj
    j_d = td % nj

    @pl.when(t == 0)
    def _():
        hb1[...] = jnp.zeros(hb1.shape, F32)

    def step(hw, hr):
        hr[0:8, :] = jnp.where(i_d == 0, 0.0, carry[j_d])
        carry[j_d] = hr[tm:tm + 8, :]

        def conv(c0, c1, cw_ref, cb_ref):
            acc = cw_ref[0:1, :] * hr[8 - 2:8 - 2 + tm, c0:c1]
            acc = acc + cw_ref[1:2, :] * hr[8 - 1:8 - 1 + tm, c0:c1]
            acc = acc + cw_ref[2:3, :] * hr[8:8 + tm, c0:c1]
            return acc + cb_ref[...]

        gate = conv(0, tn, cwg_ref, cbg_ref)
        val = conv(tn, 2 * tn, cwv_ref, cbv_ref)
        act = (gate * jax.nn.sigmoid(gate) * val).astype(BF16)
        contrib = jnp.dot(act, wd_ref[...], preferred_element_type=F32)
        o_ref[...] = jnp.where(j_d == 0, contrib, o_ref[...] + contrib)

        u = u_ref[...]
        hw[8:, 0:tn] = jnp.dot(u, wg_ref[...], preferred_element_type=F32)
        hw[8:, tn:] = jnp.dot(u, wv_ref[...], preferred_element_type=F32)

    @pl.when(t % 2 == 0)
    def _():
        step(hb0, hb1)

    @pl.when(t % 2 == 1)
    def _():
        step(hb1, hb0)


def _conv_ffn(u, w_up, conv_w, conv_b, w_down):
    s, d = u.shape
    nj = FF_PAD // MLP_TN
    n_chunks = (s // MLP_TM) * nj
    last = n_chunks - 1

    def chunk(lag):
        def ij(t):
            c = jnp.clip(t - lag, 0, last)
            return c // nj, c % nj
        return ij

    up, dn = chunk(0), chunk(1)
    return pl.pallas_call(
        functools.partial(_mlp_kernel, nj=nj),
        grid=(n_chunks + 1,),
        in_specs=[pl.BlockSpec((MLP_TM, d), lambda t: (up(t)[0], 0)),
                  pl.BlockSpec((d, MLP_TN), lambda t: (0, up(t)[1])),
                  pl.BlockSpec((d, MLP_TN), lambda t: (0, nj + up(t)[1])),
                  pl.BlockSpec((CONV_WIDTH, MLP_TN), lambda t: (0, dn(t)[1])),
                  pl.BlockSpec((CONV_WIDTH, MLP_TN), lambda t: (0, nj + dn(t)[1])),
                  pl.BlockSpec((1, MLP_TN), lambda t: (0, dn(t)[1])),
                  pl.BlockSpec((1, MLP_TN), lambda t: (0, nj + dn(t)[1])),
                  pl.BlockSpec((MLP_TN, d), lambda t: (dn(t)[1], 0))],
        out_specs=pl.BlockSpec((MLP_TM, d), lambda t: (dn(t)[0], 0)),
        out_shape=jax.ShapeDtypeStruct((s, d), F32),
        scratch_shapes=[pltpu.VMEM((8 + MLP_TM, 2 * MLP_TN), F32),
                        pltpu.VMEM((8 + MLP_TM, 2 * MLP_TN), F32),
                        pltpu.VMEM((nj, 8, 2 * MLP_TN), F32)],
        compiler_params=_params(("arbitrary",), 60),
        name="conv_ffn",
    )(u, w_up, w_up, conv_w, conv_w, conv_b, conv_b, w_down)


def _pad_ff(a, axis):
    pad = [(0, 0)] * a.ndim
    pad[axis] = (0, FF_PAD - D_FF)
    g, v = jnp.split(a, 2, axis=axis)
    return jnp.concatenate([jnp.pad(g, pad), jnp.pad(v, pad)], axis=axis)


def _cast_pad_kernel(w_ref, o_ref, *, axis, n_valid):
    c = pl.program_id(axis)

    @pl.when(c < n_valid)
    def _():
        o_ref[...] = w_ref[...].astype(o_ref.dtype)

    @pl.when(c >= n_valid)
    def _():
        o_ref[...] = jnp.zeros(o_ref.shape, o_ref.dtype)


def _cast_w_up(w_up, layer):
    d = w_up.shape[1]
    nv, npad = D_FF // CAST_BLK, FF_PAD // CAST_BLK
    return pl.pallas_call(
        functools.partial(_cast_pad_kernel, axis=1, n_valid=nv),
        grid=(2, npad),
        in_specs=[pl.BlockSpec((None, d, CAST_BLK),
                               lambda h, c: (layer, 0, h * nv + jnp.minimum(c, nv - 1)))],
        out_specs=pl.BlockSpec((d, CAST_BLK), lambda h, c: (0, h * npad + c)),
        out_shape=jax.ShapeDtypeStruct((d, 2 * FF_PAD), BF16),
        compiler_params=_params(("arbitrary", "arbitrary"), 32),
        name="cast_w_up",
    )(w_up)


def _cast_w_down(w_down, layer):
    d = w_down.shape[2]
    nv, npad = D_FF // CAST_BLK, FF_PAD // CAST_BLK
    return pl.pallas_call(
        functools.partial(_cast_pad_kernel, axis=0, n_valid=nv),
        grid=(npad,),
        in_specs=[pl.BlockSpec((None, CAST_BLK, d), lambda c: (layer, jnp.minimum(c, nv - 1), 0))],
        out_specs=pl.BlockSpec((CAST_BLK, d), lambda c: (c, 0)),
        out_shape=jax.ShapeDtypeStruct((FF_PAD, d), BF16),
        compiler_params=_params(("arbitrary",), 32),
        name="cast_w_down",
    )(w_down)


def _alibi_slopes():
    i = jnp.arange(1, N_ALIBI_HEADS + 1, dtype=F32)
    return jnp.exp2(-8.0 * i / N_ALIBI_HEADS)


def kernel(x, c, w_mod, b_mod, w_in, sinks, w_out, ln1_g, ln1_b, w_up, conv_w, conv_b, w_down, ln2_g, ln2_b):
    b, s, d = x.shape
    assert (b, s, d) == (1, SEQ, D_MODEL)
    slopes = _alibi_slopes()
    slopes_b, slopes_a = slopes[:N_Q_HEADS_B], slopes[N_Q_HEADS_B:]
    mod = _mod_all(c, w_mod, b_mod)
    x = x.reshape(s, d)
    u = None
    for l in range(DEPTH):
        shift_a, scale_a, gate_a, shift_m, scale_m, gate_m = (
            mod[l, :, n * d:(n + 1) * d] for n in range(N_MOD))
        w_up_l = _cast_w_up(w_up, l)
        w_down_l = _cast_w_down(w_down, l)
        conv_w_l = _pad_ff(conv_w[l], 1)
        conv_b_l = _pad_ff(conv_b[l].reshape(1, 2 * D_FF), 1)

        if u is None:
            u = _modulate(x, scale_a, shift_a)
        h1, h4, h16 = _in_proj_dilated(u, w_in, l, QKV_A)
        hb = _matmul([u], w_in, l, QKV_A // MM_TN, WIDTH_B + 2 * KV_WIDTH_B, BF16, "in_proj_b")
        mixed_a = _attn_a(h1, h4, h16, slopes_a)
        mixed_b = _attn_b(hb, slopes_b, sinks[l])
        att = _matmul([mixed_a, mixed_b], w_out, l, 0, d, F32, "out_proj")
        x, u = _residual_ln(x, att, gate_a, ln1_g[l].reshape(1, d), ln1_b[l].reshape(1, d),
                            (scale_m, shift_m))
        y = _conv_ffn(u, w_up_l, conv_w_l, conv_b_l, w_down_l)
        nxt = None
        if l + 1 < DEPTH:
            nxt = (mod[l + 1, :, d:2 * d], mod[l + 1, :, 0:d])
        x, u = _residual_ln(x, y, gate_m, ln2_g[l].reshape(1, d), ln2_b[l].reshape(1, d), nxt)
    return x.reshape(b, s, d)
```
